```python
import jax, jax.numpy as jnp
from jax import lax
import numpy as np

D_MODEL = 1024
BATCH = 8
SEQ = 4096
DEPTH = 2

N_A_LAYERS = DEPTH // 2
N_B_LAYERS = DEPTH - N_A_LAYERS

RWKV_HEAD_SIZE = 64
RWKV_HEADS = D_MODEL // RWKV_HEAD_SIZE
DECAY_LORA = 64
AAA_LORA = 64
GATE_LORA = 128
GN_EPS = 64e-5

MLA_HEADS = 8
QK_NOPE_DIM = 128
QK_ROPE_DIM = 64
V_HEAD_DIM = 128
Q_LORA_RANK = 512
KV_LORA_RANK = 256
ROPE_THETA = 10000.0
Q_BLOCK = 128
MAX_POS_OFFSET = 2048

D_FF = 2816
RMS_EPS = 1e-6

kernel_name = "hybrid_rwkv7_mla_yoco_macaron"


def rmsnorm(x, g):
    xf = x.astype(jnp.float32)
    y = xf * lax.rsqrt(jnp.mean(xf * xf, axis=-1, keepdims=True) + RMS_EPS)
    return (y * g.astype(jnp.float32)).astype(x.dtype)


def swiglu(x, w_gate, w_up, w_down):
    return (jax.nn.silu(x @ w_gate) * (x @ w_up)) @ w_down


def rope_tables(positions):
    inv_freq = ROPE_THETA ** (-jnp.arange(0, QK_ROPE_DIM, 2, dtype=jnp.float32) / QK_ROPE_DIM)
    ang = positions.astype(jnp.float32)[..., None] * inv_freq
    return jnp.cos(ang), jnp.sin(ang)


def apply_rope(x, cos, sin):
    half = x.shape[-1] // 2
    xf = x.astype(jnp.float32)
    x1, x2 = xf[..., :half], xf[..., half:]
    return jnp.concatenate([x1 * cos - x2 * sin, x2 * cos + x1 * sin], axis=-1).astype(x.dtype)


def wkv7_scan(r, decay, k, v, kk, a):
    B, S, H, N = r.shape

    def step(state, inp):
        r_t, w_t, k_t, v_t, kk_t, a_t = inp
        sa = jnp.einsum("bhvk,bhk->bhv", state, kk_t)
        state = (state * w_t[:, :, None, :]
                 - sa[..., None] * (kk_t * a_t)[:, :, None, :]
                 + v_t[..., None] * k_t[:, :, None, :])
        y_t = jnp.einsum("bhvk,bhk->bhv", state, r_t)
        return state, y_t

    xs = tuple(jnp.moveaxis(t, 1, 0) for t in (r, decay, k, v, kk, a))
    state0 = jnp.zeros((B, H, N, N), jnp.float32)
    _, ys = lax.scan(step, state0, xs)
    return jnp.moveaxis(ys, 0, 1)


def rwkv7_time_mix(x, mix, w_r, w_k, w_v, w_o, w0, w1, w2, a0, a1, a2, g1, g2, k_k, k_a, r_k, gn_w, gn_b):
    B, S, D = x.shape
    H, N = RWKV_HEADS, RWKV_HEAD_SIZE
    f32 = jnp.float32
    xx = jnp.pad(x, ((0, 0), (1, 0), (0, 0)))[:, :-1] - x
    xr, xw, xk, xv, xa, xg = (x + xx * mix[i] for i in range(6))
    r = xr @ w_r
    k = xk @ w_k
    v = xv @ w_v
    w_log = -jax.nn.softplus(-(w0 + jnp.tanh(xw @ w1) @ w2).astype(f32)) - 0.5
    decay = jnp.exp(-jnp.exp(w_log))
    a = jax.nn.sigmoid((a0 + (xa @ a1) @ a2).astype(f32))
    g = jax.nn.sigmoid(xg @ g1) @ g2
    kk = (k * k_k).astype(f32).reshape(B, S, H, N)
    kk = kk * lax.rsqrt(jnp.maximum(jnp.sum(kk * kk, axis=-1, keepdims=True), 1e-24))
    k = k.astype(f32) * (1.0 + (a - 1.0) * k_a.astype(f32))
    heads = lambda t: t.astype(f32).reshape(B, S, H, N)
    rh, kh, vh = heads(r), heads(k), heads(v)
    y = wkv7_scan(rh, heads(decay), kh, vh, kk, heads(a))
    mu = jnp.mean(y, axis=-1, keepdims=True)
    var = jnp.mean(jnp.square(y - mu), axis=-1, keepdims=True)
    yn = ((y - mu) * lax.rsqrt(var + GN_EPS)).reshape(B, S, D) * gn_w.astype(f32) + gn_b.astype(f32)
    bonus = jnp.sum(rh * kh * r_k.astype(f32), axis=-1, keepdims=True) * vh
    out = (yn + bonus.reshape(B, S, D)).astype(x.dtype) * g
    return out @ w_o


def mla_shared_kv(h, kv_norm_g, w_dkv, kv_latent_g, w_ukv, cos, sin):
    B, S, _ = h.shape
    ckv = rmsnorm(h, kv_norm_g) @ w_dkv
    c, k_rope = ckv[..., :KV_LORA_RANK], ckv[..., KV_LORA_RANK:]
    c = rmsnorm(c, kv_latent_g)
    kv = (c @ w_ukv).reshape(B, S, MLA_HEADS, QK_NOPE_DIM + V_HEAD_DIM)
    k_nope, v = kv[..., :QK_NOPE_DIM], kv[..., QK_NOPE_DIM:]
    k_rope = apply_rope(k_rope, cos, sin)
    return k_nope, k_rope, v


def mla_attention(x, w_dq, q_latent_g, w_uq, w_o, k_nope, k_rope, v, cos, sin):
    B, S, _ = x.shape
    q = (rmsnorm(x @ w_dq, q_latent_g) @ w_uq).reshape(B, S, MLA_HEADS, QK_NOPE_DIM + QK_ROPE_DIM)
    q_nope = q[..., :QK_NOPE_DIM]
    q_rope = apply_rope(q[..., QK_NOPE_DIM:], cos[:, :, None, :], sin[:, :, None, :])
    scale = (QK_NOPE_DIM + QK_ROPE_DIM) ** -0.5
    outs = []
    for start in range(0, S, Q_BLOCK):
        end = start + Q_BLOCK
        s = (jnp.einsum("bqhd,bkhd->bhqk", q_nope[:, start:end], k_nope[:, :end])
             + jnp.einsum("bqhd,bkd->bhqk", q_rope[:, start:end], k_rope[:, :end]))
        s = s.astype(jnp.float32) * scale
        mask = (start + jnp.arange(Q_BLOCK))[:, None] >= jnp.arange(end)[None, :]
        p = jax.nn.softmax(jnp.where(mask, s, -1e30), axis=-1).astype(v.dtype)
        outs.append(jnp.einsum("bhqk,bkhd->bqhd", p, v[:, :end]))
    o = jnp.concatenate(outs, axis=1).reshape(B, S, MLA_HEADS * V_HEAD_DIM)
    return o @ w_o


def setup_inputs(seed: int = 0) -> dict:
    key = jax.random.key(seed)
    k = jax.random.split(key, 40)
    f32 = jnp.float32
    D, H, N = D_MODEL, RWKV_HEADS, RWKV_HEAD_SIZE
    na, nb = N_A_LAYERS, N_B_LAYERS
    nrm = lambda i, shape, scale: jax.random.normal(k[i], shape, f32) * scale
    x = nrm(0, (BATCH, SEQ, D), 1.0)
    positions = (jnp.arange(SEQ, dtype=jnp.int32)[None, :]
                 + jax.random.randint(k[1], (BATCH, 1), 0, MAX_POS_OFFSET, dtype=jnp.int32))
    return {
        "x": x,
        "positions": positions,
        "norm_g": 1.0 + nrm(2, (DEPTH, 3, D), 0.02),
        "ffn_w_gate": nrm(3, (DEPTH, 2, D, D_FF), D ** -0.5),
        "ffn_w_up": nrm(4, (DEPTH, 2, D, D_FF), D ** -0.5),
        "ffn_w_down": nrm(5, (DEPTH, 2, D_FF, D), D_FF ** -0.5),
        "rwkv_mix": jax.random.uniform(k[6], (na, 6, D), f32),
        "rwkv_w_r": nrm(7, (na, D, D), D ** -0.5),
        "rwkv_w_k": nrm(8, (na, D, D), D ** -0.5),
        "rwkv_w_v": nrm(9, (na, D, D), D ** -0.5),
        "rwkv_w_o": nrm(10, (na, D, D), D ** -0.5),
        "rwkv_w0": jax.random.uniform(k[11], (na, D), f32, -3.0, 0.5),
        "rwkv_w1": nrm(12, (na, D, DECAY_LORA), D ** -0.5),
        "rwkv_w2": nrm(13, (na, DECAY_LORA, D), 0.1 * DECAY_LORA ** -0.5),
        "rwkv_a0": nrm(14, (na, D), 0.1),
        "rwkv_a1": nrm(15, (na, D, AAA_LORA), D ** -0.5),
        "rwkv_a2": nrm(16, (na, AAA_LORA, D), 0.1 * AAA_LORA ** -0.5),
        "rwkv_g1": nrm(17, (na, D, GATE_LORA), D ** -0.5),
        "rwkv_g2": nrm(18, (na, GATE_LORA, D), GATE_LORA ** -0.5),
        "rwkv_k_k": 0.85 + nrm(19, (na, D), 0.05),
        "rwkv_k_a": 1.0 + nrm(20, (na, D), 0.05),
        "rwkv_r_k": nrm(21, (na, H, N), 0.1),
        "rwkv_gn_w": 1.0 + nrm(22, (na, D), 0.02),
        "rwkv_gn_b": nrm(23, (na, D), 0.02),
        "kv_norm_g": 1.0 + nrm(24, (D,), 0.02),
        "mla_w_dkv": nrm(25, (D, KV_LORA_RANK + QK_ROPE_DIM), D ** -0.5),
        "mla_kv_latent_g": 1.0 + nrm(26, (KV_LORA_RANK,), 0.02),
        "mla_w_ukv": nrm(27, (KV_LORA_RANK, MLA_HEADS * (QK_NOPE_DIM + V_HEAD_DIM)), KV_LORA_RANK ** -0.5),
        "mla_w_dq": nrm(28, (nb, D, Q_LORA_RANK), D ** -0.5),
        "mla_q_latent_g": 1.0 + nrm(29, (nb, Q_LORA_RANK), 0.02),
        "mla_w_uq": nrm(30, (nb, Q_LORA_RANK, MLA_HEADS * (QK_NOPE_DIM + QK_ROPE_DIM)), Q_LORA_RANK ** -0.5),
        "mla_w_o": nrm(31, (nb, MLA_HEADS * V_HEAD_DIM, D), (MLA_HEADS * V_HEAD_DIM) ** -0.5),
        "final_norm_g": 1.0 + nrm(32, (D,), 0.02),
    }


def reference(x, positions, norm_g, ffn_w_gate, ffn_w_up, ffn_w_down,
              rwkv_mix, rwkv_w_r, rwkv_w_k, rwkv_w_v, rwkv_w_o, rwkv_w0, rwkv_w1, rwkv_w2,
              rwkv_a0, rwkv_a1, rwkv_a2, rwkv_g1, rwkv_g2, rwkv_k_k, rwkv_k_a, rwkv_r_k,
              rwkv_gn_w, rwkv_gn_b, kv_norm_g, mla_w_dkv, mla_kv_latent_g, mla_w_ukv,
              mla_w_dq, mla_q_latent_g, mla_w_uq, mla_w_o, final_norm_g):
    cos, sin = rope_tables(positions)
    h = x
    shared_kv = None
    for layer in range(DEPTH):
        if layer == N_A_LAYERS:
            shared_kv = mla_shared_kv(h, kv_norm_g, mla_w_dkv, mla_kv_latent_g, mla_w_ukv, cos, sin)
        h = h + 0.5 * swiglu(rmsnorm(h, norm_g[layer, 0]),
                             ffn_w_gate[layer, 0], ffn_w_up[layer, 0], ffn_w_down[layer, 0])
        hn = rmsnorm(h, norm_g[layer, 1])
        if layer < N_A_LAYERS:
            i = layer
            h = h + rwkv7_time_mix(hn, rwkv_mix[i], rwkv_w_r[i], rwkv_w_k[i], rwkv_w_v[i], rwkv_w_o[i],
                                   rwkv_w0[i], rwkv_w1[i], rwkv_w2[i], rwkv_a0[i], rwkv_a1[i], rwkv_a2[i],
                                   rwkv_g1[i], rwkv_g2[i], rwkv_k_k[i], rwkv_k_a[i], rwkv_r_k[i],
                                   rwkv_gn_w[i], rwkv_gn_b[i])
        else:
            j = layer - N_A_LAYERS
            k_nope, k_rope, v = shared_kv
            h = h + mla_attention(hn, mla_w_dq[j], mla_q_latent_g[j], mla_w_uq[j], mla_w_o[j],
                                  k_nope, k_rope, v, cos, sin)
        h = h + 0.5 * swiglu(rmsnorm(h, norm_g[layer, 2]),
                             ffn_w_gate[layer, 1], ffn_w_up[layer, 1], ffn_w_down[layer, 1])
    return rmsnorm(h, final_norm_g)
```

```python
import functools

import numpy as np
import jax
import jax.numpy as jnp
from jax import lax
from jax.experimental import pallas as pl
from jax.experimental.pallas import tpu as pltpu

F32, BF16 = jnp.float32, jnp.bfloat16

RWKV_HEAD = 64
GN_EPS = 64e-5
RMS_EPS = 1e-6
MLA_HEADS = 8
QK_NOPE = 128
QK_ROPE = 64
V_HEAD = 128
KV_LORA = 256
ROPE_THETA = 10000.0

LANES = 128
MXU_DIM = 256
VMEM_LIMIT = 48 * 1024 * 1024

TOKEN_TILE = 512
WKV_CHUNK = 64
WKV_GROUP = MXU_DIM // RWKV_HEAD
ATTN_TILE = 512


def _params(*sem):
    return pltpu.CompilerParams(dimension_semantics=sem, vmem_limit_bytes=VMEM_LIMIT)


def _resident(shape):
    return pl.BlockSpec(shape, lambda *_: (0,) * len(shape), pipeline_mode=pl.Buffered(1))


def _rows(tm, d):
    return pl.BlockSpec((tm, d), lambda i: (i, 0))


def _dot(a, b):
    return jnp.dot(a.astype(BF16), b.astype(BF16), preferred_element_type=F32)


def _dot_nt(a, b):
    return lax.dot_general(a.astype(BF16), b.astype(BF16), (((1,), (1,)), ((), ())),
                           preferred_element_type=F32)


def _dot_hilo(a, b):
    hi = a.astype(BF16)
    lo = (a - hi.astype(F32)).astype(BF16)
    return (jnp.dot(hi, b, preferred_element_type=F32) + jnp.dot(lo, b, preferred_element_type=F32))


def _rms(x, g):
    return x * lax.rsqrt(jnp.mean(x * x, axis=-1, keepdims=True) + RMS_EPS) * g


def _ffn_body(*refs, n_chunks, pre_proj, final):
    refs = list(refs)
    x_ref = refs.pop(0)
    x = x_ref[...]
    if pre_proj:
        o_ref_in, wo_ref = refs.pop(0), refs.pop(0)
        x = x + jnp.dot(o_ref_in[...], wo_ref[...], preferred_element_type=F32)
    g_ref, wg_ref, wu_ref, wd_ref = refs[:4]
    out_ref = refs[-1]
    xn = _rms(x, g_ref[...]).astype(BF16)
    fc = wg_ref.shape[1] // n_chunks
    acc = jnp.zeros_like(x)
    for c in range(n_chunks):
        sl = slice(c * fc, (c + 1) * fc)
        gate = jnp.dot(xn, wg_ref[:, sl], preferred_element_type=F32)
        up = jnp.dot(xn, wu_ref[:, sl], preferred_element_type=F32)
        act = (gate * jax.nn.sigmoid(gate) * up).astype(BF16)
        acc = acc + jnp.dot(act, wd_ref[sl, :], preferred_element_type=F32)
    y = x + 0.5 * acc
    if final:
        y = _rms(y, refs[4][...])
    out_ref[...] = y


def _ffn(x, g, wg, wu, wd, *, pre=None, final_g=None):
    m, d = x.shape
    f = wg.shape[1]
    tm = TOKEN_TILE
    ins, specs = [x], [_rows(tm, d)]
    if pre is not None:
        o, wo = pre
        ins += [o, wo]
        specs += [_rows(tm, o.shape[1]), _resident(wo.shape)]
    ins += [g, wg, wu, wd]
    specs += [_resident((1, d)), _resident((d, f)), _resident((d, f)), _resident((f, d))]
    if final_g is not None:
        ins.append(final_g)
        specs.append(_resident((1, d)))
    body = functools.partial(_ffn_body, n_chunks=2, pre_proj=pre is not None, final=final_g is not None)
    return pl.pallas_call(
        body, grid=(m // tm,), in_specs=specs, out_specs=_rows(tm, d),
        out_shape=jax.ShapeDtypeStruct((m, d), F32), compiler_params=_params("parallel"),
        name="ffn")(*ins)


def _softplus(z):
    return jnp.maximum(z, 0.0) + jnp.log(1.0 + jnp.exp(-jnp.abs(z)))


def _rwkv_proj_body(h_ref, hp_ref, g_ref, mix_ref, wr_ref, wk_ref, wv_ref, w0_ref, w1_ref, w2_ref,
                    a0_ref, a1_ref, a2_ref, g1_ref, g2_ref, kk_ref, ka_ref, rk_ref, e_ref, et_ref,
                    r_out, k_out, v_out, kkn_out, b_out, lw_out, gate_out, bonus_out, *, tiles_per_seq):
    i = pl.program_id(0)
    gn = g_ref[...]
    hn = _rms(h_ref[...], gn)
    tm = hn.shape[0]
    prev = _rms(hp_ref[...], gn)[7:8, :]
    prev = jnp.where(i % tiles_per_seq == 0, 0.0, prev)
    shifted = pltpu.roll(hn, 1, 0)
    row = lax.broadcasted_iota(jnp.int32, (tm, 1), 0)
    shifted = jnp.where(row == 0, prev, shifted)
    xx = shifted - hn
    xr, xw, xk, xv, xa, xg = (hn + xx * mix_ref[c:c + 1, :] for c in range(6))
    r = _dot(xr, wr_ref[...])
    k = _dot(xk, wk_ref[...])
    v = _dot(xv, wv_ref[...])
    wl = w0_ref[...] + _dot(jnp.tanh(_dot(xw, w1_ref[...])), w2_ref[...])
    w_log = -_softplus(-wl) - 0.5
    lw_out[...] = -jnp.exp(w_log)
    a = jax.nn.sigmoid(a0_ref[...] + _dot(_dot(xa, a1_ref[...]), a2_ref[...]))
    gate_out[...] = _dot(jax.nn.sigmoid(_dot(xg, g1_ref[...])), g2_ref[...]).astype(BF16)
    e, et = e_ref[...], et_ref[...]
    kk = k * kk_ref[...]
    inv = lax.rsqrt(jnp.maximum(_dot(kk * kk, e), 1e-24))
    kk = kk * _dot_hilo(inv, et)
    k = k * (1.0 + (a - 1.0) * ka_ref[...])
    r_out[...] = r
    k_out[...] = k
    v_out[...] = v.astype(BF16)
    kkn_out[...] = kk
    b_out[...] = kk * a
    bonus_out[...] = _dot(_dot(r * k * rk_ref[...], e), et) * v


def _rwkv_proj(h, g, p, seq):
    m, d = h.shape
    tm = TOKEN_TILE
    lora_specs = [_resident(p[n].shape) for n in ("w0", "w1", "w2", "a0", "a1", "a2", "g1", "g2")]
    vec = _resident((1, d))
    in_specs = ([_rows(tm, d), pl.BlockSpec((8, d), lambda i: (jnp.maximum(i * (tm // 8) - 1, 0), 0)),
                 vec, _resident((6, d)), _resident((d, d)), _resident((d, d)), _resident((d, d))]
                + lora_specs + [vec, vec, vec, _resident(p["e"].shape), _resident(p["et"].shape)])
    f32o = jax.ShapeDtypeStruct((m, d), F32)
    bf16o = jax.ShapeDtypeStruct((m, d), BF16)
    body = functools.partial(_rwkv_proj_body, tiles_per_seq=seq // tm)
    return pl.pallas_call(
        body, grid=(m // tm,), in_specs=in_specs, out_specs=[_rows(tm, d)] * 8,
        out_shape=[f32o, f32o, bf16o, f32o, f32o, f32o, bf16o, f32o],
        compiler_params=_params("parallel"), name="rwkv_proj")(
        h, h, g, p["mix"], p["wr"], p["wk"], p["wv"], p["w0"], p["w1"], p["w2"], p["a0"], p["a1"], p["a2"],
        p["g1"], p["g2"], p["k_k"], p["k_a"], p["r_k"], p["e"], p["et"])


def _blk(x, blkmask):
    return jnp.concatenate([x] * WKV_GROUP, axis=0) * blkmask


def _wkv_body(lw_ref, r_ref, k_ref, v_ref, kk_ref, b_ref, blk_ref, y_ref, h_ref):
    c = WKV_CHUNK
    gw = MXU_DIM

    @pl.when(pl.program_id(1) == 0)
    def _():
        h_ref[...] = jnp.zeros_like(h_ref)

    blkmask = blk_ref[...]
    blkmask_f = blkmask.astype(F32)
    row = lax.broadcasted_iota(jnp.int32, (c, gw), 0)
    colj = lax.broadcasted_iota(jnp.int32, (c, gw), 1) % c
    strict, incl = row > colj, row >= colj
    eye_w = (row == colj).astype(F32)
    r2 = lax.broadcasted_iota(jnp.int32, (c, c), 0)
    c2 = lax.broadcasted_iota(jnp.int32, (c, c), 1)
    tri = (r2 >= c2).astype(BF16)

    lw = lw_ref[...]
    cs = _dot_hilo_left(tri, lw)
    cs_end = cs[c - 1:c, :]
    w_in = jnp.exp(cs)
    w_inv = jnp.exp(-cs)
    w_rem = jnp.exp(cs_end - cs)
    w_end = jnp.exp(cs_end)
    kk = kk_ref[...]
    a_t = (-kk * jnp.exp(cs - lw)).astype(BF16)
    r_t_f = r_ref[...] * w_in
    r_t = r_t_f.astype(BF16)
    b_raw = b_ref[...]
    k_raw = k_ref[...]
    b_t = (b_raw * w_inv).astype(BF16)
    k_t = (k_raw * w_inv).astype(BF16)
    b_c = (b_raw * w_rem).astype(BF16)
    k_c = (k_raw * w_rem).astype(BF16)
    v_all = v_ref[...]

    for g in range(h_ref.shape[1] // gw):
        sl = slice(g * gw, (g + 1) * gw)
        vg = v_all[:, sl]
        vblk = _blk(vg, blkmask)
        ar = jnp.concatenate([a_t[:, sl], r_t[:, sl]], axis=0)
        sb = _dot_nt(ar, _blk(b_t[:, sl], blkmask))
        sk = _dot_nt(ar, _blk(k_t[:, sl], blkmask))
        lab = jnp.where(strict, sb[:c], 0.0)
        mrb = jnp.where(incl, sb[c:], 0.0).astype(BF16)
        lak = jnp.where(strict, sk[:c], 0.0).astype(BF16)
        mrk = jnp.where(incl, sk[c:], 0.0).astype(BF16)
        x = lab.astype(BF16)
        t = eye_w + lab
        xb = _blk(x, blkmask)
        for _ in range(5):
            x = jnp.dot(x, xb, preferred_element_type=F32).astype(BF16)
            xb = _blk(x, blkmask)
            t = t + jnp.dot(t.astype(BF16), xb, preferred_element_type=F32)
        lv = jnp.dot(lak, vblk, preferred_element_type=F32).astype(BF16)
        pq = jnp.dot(t.astype(BF16), jnp.concatenate([_blk(a_t[:, sl], blkmask), _blk(lv, blkmask)], axis=1),
                     preferred_element_type=F32).astype(BF16)
        p_w, q_w = pq[:, :gw], pq[:, gw:]
        ry = jnp.dot(mrb, jnp.concatenate([_blk(p_w, blkmask), _blk(q_w, blkmask)], axis=1),
                     preferred_element_type=F32)
        r_hat = r_t_f[:, sl] + ry[:, :gw]
        y_hat = ry[:, gw:] + jnp.dot(mrk, vblk, preferred_element_type=F32)
        bk = jnp.concatenate([b_c[:, sl], k_c[:, sl]], axis=0)
        rhs = jnp.concatenate([pq, jnp.concatenate([jnp.zeros_like(vg), vg], axis=1)], axis=0)
        gz = lax.dot_general(bk, rhs, (((0,), (0,)), ((), ())), preferred_element_type=F32)
        gfull = gz[:, :gw] * blkmask_f
        zfull = gz[:, gw:] * blkmask_f
        g_w = eye_w * w_end[:, sl] + sum(gfull[j * c:(j + 1) * c] for j in range(WKV_GROUP))
        z_w = sum(zfull[j * c:(j + 1) * c] for j in range(WKV_GROUP))
        hb = _blk(h_ref[:, sl].astype(BF16), blkmask)
        rg = jnp.dot(jnp.concatenate([r_hat, g_w], axis=0).astype(BF16), hb, preferred_element_type=F32)
        y_ref[:, sl] = rg[:c] + y_hat
        h_ref[:, sl] = rg[c:] + z_w


def _dot_hilo_left(a, b):
    hi = b.astype(BF16)
    lo = (b - hi.astype(F32)).astype(BF16)
    return jnp.dot(a, hi, preferred_element_type=F32) + jnp.dot(a, lo, preferred_element_type=F32)


def _wkv(lw, r, k, v, kk, b, blkmask, batch, seq):
    m, d = r.shape
    c = WKV_CHUNK
    nc = seq // c
    spec = pl.BlockSpec((c, d), lambda bi, ci: (bi * nc + ci, 0))
    return pl.pallas_call(
        _wkv_body, grid=(batch, nc),
        in_specs=[spec] * 6 + [_resident(blkmask.shape)], out_specs=spec,
        out_shape=jax.ShapeDtypeStruct((m, d), F32),
        scratch_shapes=[pltpu.VMEM((RWKV_HEAD, d), F32)],
        compiler_params=_params("parallel", "arbitrary"), name="wkv")(lw, r, k, v, kk, b, blkmask)


def _rwkv_out_body(y_ref, bonus_ref, gate_ref, h_ref, gnw_ref, gnb_ref, wo_ref, e_ref, et_ref, out_ref):
    e, et = e_ref[...], et_ref[...]
    y = y_ref[...]
    inv_n = 1.0 / RWKV_HEAD
    mu = _dot_hilo(_dot(y, e) * inv_n, et)
    dlt = y - mu
    var = _dot(dlt * dlt, e) * inv_n
    rstd = _dot_hilo(lax.rsqrt(var + GN_EPS), et)
    yn = dlt * rstd * gnw_ref[...] + gnb_ref[...]
    out = (yn + bonus_ref[...]) * gate_ref[...].astype(F32)
    out_ref[...] = h_ref[...] + _dot(out, wo_ref[...])


def _rwkv_out(y, bonus, gate, h, p):
    m, d = h.shape
    tm = TOKEN_TILE
    vec = _resident((1, d))
    return pl.pallas_call(
        _rwkv_out_body, grid=(m // tm,),
        in_specs=[_rows(tm, d)] * 4 + [vec, vec, _resident((d, d)), _resident(p["e"].shape), _resident(p["et"].shape)],
        out_specs=_rows(tm, d), out_shape=jax.ShapeDtypeStruct((m, d), F32),
        compiler_params=_params("parallel"), name="rwkv_out")(
        y, bonus, gate, h, p["gn_w"], p["gn_b"], p["wo"], p["e"], p["et"])


def _rope_body(pos_ref, freq_ref, sign_ref, cos_ref, sin_ref):
    ang = pos_ref[...].astype(F32) * freq_ref[...]
    cos_ref[...] = jnp.cos(ang)
    sin_ref[...] = jnp.sin(ang) * sign_ref[...]


def _rope_tables(pos, freq, sign):
    m = pos.shape[0]
    tm = TOKEN_TILE
    out = jax.ShapeDtypeStruct((m, LANES), F32)
    return pl.pallas_call(
        _rope_body, grid=(m // tm,),
        in_specs=[_rows(tm, 1), _resident((1, LANES)), _resident((1, LANES))],
        out_specs=[_rows(tm, LANES)] * 2, out_shape=[out, out],
        compiler_params=_params("parallel"), name="rope_tables")(pos, freq, sign)


def _mla_kv_body(h_ref, g_ref, wd_ref, lg_ref, wu_ref, cos_ref, sin_ref, kn_out, kr_out, v_out):
    hn = _rms(h_ref[...], g_ref[...])
    ckv = _dot(hn, wd_ref[...])
    lat = _rms(ckv[:, :KV_LORA], lg_ref[...])
    kv = _dot(lat, wu_ref[...])
    half = kv.shape[1] // 2
    kn_out[...] = kv[:, :half].astype(BF16)
    v_out[...] = kv[:, half:].astype(BF16)
    kr = ckv[:, KV_LORA:KV_LORA + LANES]
    krs = ckv[:, KV_LORA + LANES:]
    kr_out[...] = (kr * cos_ref[...] + krs * sin_ref[...]).astype(BF16)


def _mla_kv(h, p, cos, sin):
    m, d = h.shape
    tm = TOKEN_TILE
    hv = MLA_HEADS * V_HEAD
    return pl.pallas_call(
        _mla_kv_body, grid=(m // tm,),
        in_specs=[_rows(tm, d), _resident((1, d)), _resident(p["w_dkv"].shape), _resident((1, KV_LORA)),
                  _resident(p["w_ukv"].shape), _rows(tm, LANES), _rows(tm, LANES)],
        out_specs=[_rows(tm, hv), _rows(tm, LANES), _rows(tm, hv)],
        out_shape=[jax.ShapeDtypeStruct((m, hv), BF16), jax.ShapeDtypeStruct((m, LANES), BF16),
                   jax.ShapeDtypeStruct((m, hv), BF16)],
        compiler_params=_params("parallel"), name="mla_kv")(
        h, p["kv_norm_g"], p["w_dkv"], p["kv_latent_g"], p["w_ukv"], cos, sin)


def _mla_q_body(h_ref, g_ref, wd_ref, lg_ref, wu_ref, cos_ref, sin_ref, qn_out, qr_out, *, scale):
    hn = _rms(h_ref[...], g_ref[...])
    lat = _rms(_dot(hn, wd_ref[...]), lg_ref[...])
    q = _dot(lat, wu_ref[...]) * scale
    w = q.shape[1] // 3
    cos = jnp.concatenate([cos_ref[...]] * MLA_HEADS, axis=1)
    sin = jnp.concatenate([sin_ref[...]] * MLA_HEADS, axis=1)
    qn_out[...] = q[:, :w].astype(BF16)
    qr_out[...] = (q[:, w:2 * w] * cos + q[:, 2 * w:] * sin).astype(BF16)


def _mla_q(h, g, p, cos, sin):
    m, d = h.shape
    tm = TOKEN_TILE
    w = MLA_HEADS * LANES
    scale = float(QK_NOPE + QK_ROPE) ** -0.5
    out = jax.ShapeDtypeStruct((m, w), BF16)
    return pl.pallas_call(
        functools.partial(_mla_q_body, scale=scale), grid=(m // tm,),
        in_specs=[_rows(tm, d), _resident((1, d)), _resident(p["w_dq"].shape), _resident(p["q_latent_g"].shape),
                  _resident(p["w_uq"].shape), _rows(tm, LANES), _rows(tm, LANES)],
        out_specs=[_rows(tm, w)] * 2, out_shape=[out, out],
        compiler_params=_params("parallel"), name="mla_q")(
        h, g, p["w_dq"], p["q_latent_g"], p["w_uq"], cos, sin)


def _attn_body(qn_ref, qr_ref, kn_ref, kr_ref, v_ref, o_ref):
    t = ATTN_TILE
    qi = pl.program_id(2)
    q = jnp.concatenate([qn_ref[...], qr_ref[...]], axis=1)

    def step(j, carry, masked):
        m_i, l_i, acc = carry
        off = pl.multiple_of(j * t, t)
        k = jnp.concatenate([kn_ref[pl.ds(off, t), :], kr_ref[pl.ds(off, t), :]], axis=1)
        s = lax.dot_general(q, k, (((1,), (1,)), ((), ())), preferred_element_type=F32)
        if masked:
            row = lax.broadcasted_iota(jnp.int32, (t, t), 0)
            col = lax.broadcasted_iota(jnp.int32, (t, t), 1)
            s = jnp.where(row >= col, s, -1e30)
        m_new = jnp.maximum(m_i, jnp.max(s, axis=-1, keepdims=True))
        p = jnp.exp(s - m_new)
        alpha = jnp.exp(m_i - m_new)
        l_new = alpha * l_i + jnp.sum(p, axis=-1, keepdims=True)
        acc = alpha * acc + jnp.dot(p.astype(BF16), v_ref[pl.ds(off, t), :], preferred_element_type=F32)
        return m_new, l_new, acc

    init = (jnp.full((t, 1), -jnp.inf, F32), jnp.zeros((t, 1), F32), jnp.zeros((t, V_HEAD), F32))
    carry = lax.fori_loop(0, qi, functools.partial(step, masked=False), init)
    _, l_i, acc = step(qi, carry, True)
    o_ref[...] = (acc / l_i).astype(BF16)


def _attention(qn, qr, kn, kr, v, batch, seq):
    m = qn.shape[0]
    t = ATTN_TILE
    nq = seq // t
    qspec = pl.BlockSpec((t, LANES), lambda b, h, i: (b * nq + i, h))
    kspec = pl.BlockSpec((seq, LANES), lambda b, h, i: (b, h))
    krspec = pl.BlockSpec((seq, LANES), lambda b, h, i: (b, 0))
    return pl.pallas_call(
        _attn_body, grid=(batch, MLA_HEADS, nq),
        in_specs=[qspec, qspec, kspec, krspec, kspec], out_specs=qspec,
        out_shape=jax.ShapeDtypeStruct((m, MLA_HEADS * V_HEAD), BF16),
        compiler_params=_params("parallel", "parallel", "arbitrary"), name="mla_attention")(qn, qr, kn, kr, v)


def _prep_rwkv(mix, w_r, w_k, w_v, w_o, w0, w1, w2, a0, a1, a2, g1, g2, k_k, k_a, r_k, gn_w, gn_b):
    d = w_r.shape[0]
    pad_c = lambda w: jnp.pad(w, ((0, 0), (0, LANES - w.shape[1]))).astype(BF16)
    pad_r = lambda w: jnp.pad(w, ((0, LANES - w.shape[0]), (0, 0))).astype(BF16)
    head_of = np.arange(d) // RWKV_HEAD
    e = (head_of[:, None] == np.arange(LANES)[None, :]).astype(np.float32)
    return dict(
        mix=mix, wr=w_r.astype(BF16), wk=w_k.astype(BF16), wv=w_v.astype(BF16), wo=w_o.astype(BF16),
        w0=w0[None, :], w1=pad_c(w1), w2=pad_r(w2), a0=a0[None, :], a1=pad_c(a1), a2=pad_r(a2),
        g1=g1.astype(BF16), g2=g2.astype(BF16), k_k=k_k[None, :], k_a=k_a[None, :], r_k=r_k.reshape(1, d),
        gn_w=gn_w[None, :], gn_b=gn_b[None, :], e=jnp.asarray(e, BF16), et=jnp.asarray(e.T, BF16))


def _prep_mla(kv_norm_g, w_dkv, kv_latent_g, w_ukv, w_dq, q_latent_g, w_uq, w_o):
    half = QK_ROPE // 2
    zpad = LANES - QK_ROPE
    x1 = w_dkv[:, KV_LORA:KV_LORA + half]
    x2 = w_dkv[:, KV_LORA + half:]
    z = jnp.zeros((w_dkv.shape[0], zpad), w_dkv.dtype)
    w_dkv_p = jnp.concatenate([w_dkv[:, :KV_LORA], x1, x2, z, x2, x1, z], axis=1)
    ukv = w_ukv.reshape(KV_LORA, MLA_HEADS, QK_NOPE + V_HEAD)
    w_ukv_p = jnp.concatenate([ukv[:, :, :QK_NOPE].reshape(KV_LORA, -1), ukv[:, :, QK_NOPE:].reshape(KV_LORA, -1)], axis=1)
    uq = w_uq.reshape(w_uq.shape[0], MLA_HEADS, QK_NOPE + QK_ROPE)
    q1 = uq[:, :, QK_NOPE:QK_NOPE + half]
    q2 = uq[:, :, QK_NOPE + half:]
    zq = jnp.zeros(q1.shape[:2] + (zpad,), w_uq.dtype)
    flat = lambda t: t.reshape(t.shape[0], -1)
    w_uq_p = jnp.concatenate([flat(uq[:, :, :QK_NOPE]), flat(jnp.concatenate([q1, q2, zq], axis=2)),
                              flat(jnp.concatenate([q2, q1, zq], axis=2))], axis=1)
    return dict(kv_norm_g=kv_norm_g[None, :], w_dkv=w_dkv_p.astype(BF16), kv_latent_g=kv_latent_g[None, :],
                w_ukv=w_ukv_p.astype(BF16), w_dq=w_dq.astype(BF16), q_latent_g=q_latent_g[None, :],
                w_uq=w_uq_p.astype(BF16), w_o=w_o.astype(BF16))


def _rope_consts():
    half = QK_ROPE // 2
    inv_freq = ROPE_THETA ** (-jnp.arange(0, QK_ROPE, 2, dtype=F32) / QK_ROPE)
    reps = LANES // half
    freq = jnp.tile(inv_freq, reps)[None, :]
    sign = jnp.tile(jnp.concatenate([-jnp.ones((half,), F32), jnp.ones((half,), F32)]), reps // 2)[None, :]
    return freq, sign


def kernel(x, positions, norm_g, ffn_w_gate, ffn_w_up, ffn_w_down, rwkv_mix, rwkv_w_r, rwkv_w_k, rwkv_w_v, rwkv_w_o, rwkv_w0, rwkv_w1, rwkv_w2, rwkv_a0, rwkv_a1, rwkv_a2, rwkv_g1, rwkv_g2, rwkv_k_k, rwkv_k_a, rwkv_r_k, rwkv_gn_w, rwkv_gn_b, kv_norm_g, mla_w_dkv, mla_kv_latent_g, mla_w_ukv, mla_w_dq, mla_q_latent_g, mla_w_uq, mla_w_o, final_norm_g):
    batch, seq, d = x.shape
    depth = norm_g.shape[0]
    n_a = rwkv_mix.shape[0]
    m = batch * seq
    assert seq % TOKEN_TILE == 0 and seq % ATTN_TILE == 0 and seq % WKV_CHUNK == 0 and d % MXU_DIM == 0

    wg, wu, wd = ffn_w_gate.astype(BF16), ffn_w_up.astype(BF16), ffn_w_down.astype(BF16)
    blkmask = jnp.asarray(np.kron(np.eye(WKV_GROUP), np.ones((RWKV_HEAD, RWKV_HEAD))), BF16)
    freq, sign = _rope_consts()
    cos, sin = _rope_tables(positions.reshape(m, 1), freq, sign)
    mla = None

    h = x.reshape(m, d)
    kv = None
    for layer in range(depth):
        norm = lambda j: norm_g[layer, j][None, :]
        if layer == n_a:
            mla = _prep_mla(kv_norm_g, mla_w_dkv, mla_kv_latent_g, mla_w_ukv, mla_w_dq[0], mla_q_latent_g[0],
                            mla_w_uq[0], mla_w_o[0])
            kv = _mla_kv(h, mla, cos, sin)
        h = _ffn(h, norm(0), wg[layer, 0], wu[layer, 0], wd[layer, 0])
        last = layer == depth - 1
        if layer < n_a:
            i = layer
            p = _prep_rwkv(rwkv_mix[i], rwkv_w_r[i], rwkv_w_k[i], rwkv_w_v[i], rwkv_w_o[i], rwkv_w0[i], rwkv_w1[i],
                           rwkv_w2[i], rwkv_a0[i], rwkv_a1[i], rwkv_a2[i], rwkv_g1[i], rwkv_g2[i], rwkv_k_k[i],
                           rwkv_k_a[i], rwkv_r_k[i], rwkv_gn_w[i], rwkv_gn_b[i])
            r, k, v, kk, b, lw, gate, bonus = _rwkv_proj(h, norm(1), p, seq)
            y = _wkv(lw, r, k, v, kk, b, blkmask, batch, seq)
            h = _rwkv_out(y, bonus, gate, h, p)
            pre = None
        else:
            j = layer - n_a
            if j > 0:
                mla = dict(mla, **{n: v_ for n, v_ in _prep_mla(
                    kv_norm_g, mla_w_dkv, mla_kv_latent_g, mla_w_ukv, mla_w_dq[j], mla_q_latent_g[j], mla_w_uq[j],
                    mla_w_o[j]).items() if n in ("w_dq", "q_latent_g", "w_uq", "w_o")})
            qn, qr = _mla_q(h, norm(1), mla, cos, sin)
            o = _attention(qn, qr, kv[0], kv[1], kv[2], batch, seq)
            pre = (o, mla["w_o"])
        h = _ffn(h, norm(2), wg[layer, 1], wu[layer, 1], wd[layer, 1], pre=pre,
                 final_g=final_norm_g[None, :] if last else None)
    return h.reshape(batch, seq, d)
```

```python
import functools

import numpy as np
import jax
import jax.numpy as jnp
from jax import lax
from jax.experimental import pallas as pl
from jax.experimental.pallas import tpu as pltpu

F32, BF16 = jnp.float32, jnp.bfloat16

RWKV_HEAD = 64
GN_EPS = 64e-5
RMS_EPS = 1e-6
MLA_HEADS = 8
QK_NOPE = 128
QK_ROPE = 64
V_HEAD = 128
KV_LORA = 256
ROPE_THETA = 10000.0

LANES = 128
MXU_DIM = 256
VMEM_LIMIT = 48 * 1024 * 1024

TOKEN_TILE = 512
WKV_CHUNK = 64
WKV_GROUP = MXU_DIM // RWKV_HEAD
WKV_SUBCHUNKS = 4
ATTN_TILE = 1024
ATTN_CHAIN = 256


def _params(*sem):
    return pltpu.CompilerParams(dimension_semantics=sem, vmem_limit_bytes=VMEM_LIMIT)


def _resident(shape):
    return pl.BlockSpec(shape, lambda *_: (0,) * len(shape), pipeline_mode=pl.Buffered(1))


def _rows(tm, d):
    return pl.BlockSpec((tm, d), lambda i: (i, 0))


def _dot(a, b):
    return jnp.dot(a.astype(BF16), b.astype(BF16), preferred_element_type=F32)


def _dot_nt(a, b):
    return lax.dot_general(a.astype(BF16), b.astype(BF16), (((1,), (1,)), ((), ())),
                           preferred_element_type=F32)


def _dot_hilo(a, b):
    hi = a.astype(BF16)
    lo = (a - hi.astype(F32)).astype(BF16)
    return (jnp.dot(hi, b, preferred_element_type=F32) + jnp.dot(lo, b, preferred_element_type=F32))


def _rms(x, g):
    return x * lax.rsqrt(jnp.mean(x * x, axis=-1, keepdims=True) + RMS_EPS) * g


def _ffn_body(*refs, n_chunks, pre_proj, final):
    refs = list(refs)
    x_ref = refs.pop(0)
    x = x_ref[...]
    if pre_proj:
        o_ref_in, wo_ref = refs.pop(0), refs.pop(0)
        x = x + jnp.dot(o_ref_in[...], wo_ref[...], preferred_element_type=F32)
    g_ref, wg_ref, wu_ref, wd_ref = refs[:4]
    out_ref = refs[-1]
    xn = _rms(x, g_ref[...]).astype(BF16)
    fc = wg_ref.shape[1] // n_chunks
    acc = jnp.zeros_like(x)
    for c in range(n_chunks):
        sl = slice(c * fc, (c + 1) * fc)
        gate = jnp.dot(xn, wg_ref[:, sl], preferred_element_type=F32)
        up = jnp.dot(xn, wu_ref[:, sl], preferred_element_type=F32)
        act = (gate * jax.nn.sigmoid(gate) * up).astype(BF16)
        acc = acc + jnp.dot(act, wd_ref[sl, :], preferred_element_type=F32)
    y = x + 0.5 * acc
    if final:
        y = _rms(y, refs[4][...])
    out_ref[...] = y


def _ffn(x, g, wg, wu, wd, *, pre=None, final_g=None):
    m, d = x.shape
    f = wg.shape[1]
    tm = TOKEN_TILE
    ins, specs = [x], [_rows(tm, d)]
    if pre is not None:
        o, wo = pre
        ins += [o, wo]
        specs += [_rows(tm, o.shape[1]), _resident(wo.shape)]
    ins += [g, wg, wu, wd]
    specs += [_resident((1, d)), _resident((d, f)), _resident((d, f)), _resident((f, d))]
    if final_g is not None:
        ins.append(final_g)
        specs.append(_resident((1, d)))
    body = functools.partial(_ffn_body, n_chunks=2, pre_proj=pre is not None, final=final_g is not None)
    return pl.pallas_call(
        body, grid=(m // tm,), in_specs=specs, out_specs=_rows(tm, d),
        out_shape=jax.ShapeDtypeStruct((m, d), F32), compiler_params=_params("parallel"),
        name="ffn")(*ins)


def _softplus(z):
    return jnp.maximum(z, 0.0) + jnp.log(1.0 + jnp.exp(-jnp.abs(z)))


def _rwkv_proj_body(h_ref, hp_ref, g_ref, mix_ref, wr_ref, wk_ref, wv_ref, w0_ref, w1_ref, w2_ref,
                    a0_ref, a1_ref, a2_ref, g1_ref, g2_ref, kk_ref, ka_ref, rk_ref, e_ref, et_ref,
                    r_out, k_out, v_out, kkn_out, b_out, lw_out, gate_out, bonus_out, *, tiles_per_seq):
    i = pl.program_id(0)
    gn = g_ref[...]
    hn = _rms(h_ref[...], gn)
    tm = hn.shape[0]
    prev = _rms(hp_ref[...], gn)[7:8, :]
    prev = jnp.where(i % tiles_per_seq == 0, 0.0, prev)
    shifted = pltpu.roll(hn, 1, 0)
    row = lax.broadcasted_iota(jnp.int32, (tm, 1), 0)
    shifted = jnp.where(row == 0, prev, shifted)
    xx = shifted - hn
    xr, xw, xk, xv, xa, xg = (hn + xx * mix_ref[c:c + 1, :] for c in range(6))
    r = _dot(xr, wr_ref[...])
    k = _dot(xk, wk_ref[...])
    v = _dot(xv, wv_ref[...])
    wl = w0_ref[...] + _dot(jnp.tanh(_dot(xw, w1_ref[...])), w2_ref[...])
    w_log = -_softplus(-wl) - 0.5
    lw_out[...] = -jnp.exp(w_log)
    a = jax.nn.sigmoid(a0_ref[...] + _dot(_dot(xa, a1_ref[...]), a2_ref[...]))
    gate_out[...] = _dot(jax.nn.sigmoid(_dot(xg, g1_ref[...])), g2_ref[...]).astype(BF16)
    e, et = e_ref[...], et_ref[...]
    kk = k * kk_ref[...]
    inv = lax.rsqrt(jnp.maximum(_dot(kk * kk, e), 1e-24))
    kk = kk * _dot_hilo(inv, et)
    k = k * (1.0 + (a - 1.0) * ka_ref[...])
    r_out[...] = r
    k_out[...] = k
    v_out[...] = v.astype(BF16)
    kkn_out[...] = kk
    b_out[...] = kk * a
    bonus_out[...] = _dot(_dot(r * k * rk_ref[...], e), et) * v


def _rwkv_proj(h, g, p, seq):
    m, d = h.shape
    tm = TOKEN_TILE
    lora_specs = [_resident(p[n].shape) for n in ("w0", "w1", "w2", "a0", "a1", "a2", "g1", "g2")]
    vec = _resident((1, d))
    in_specs = ([_rows(tm, d), pl.BlockSpec((8, d), lambda i: (jnp.maximum(i * (tm // 8) - 1, 0), 0)),
                 vec, _resident((6, d)), _resident((d, d)), _resident((d, d)), _resident((d, d))]
                + lora_specs + [vec, vec, vec, _resident(p["e"].shape), _resident(p["et"].shape)])
    f32o = jax.ShapeDtypeStruct((m, d), F32)
    bf16o = jax.ShapeDtypeStruct((m, d), BF16)
    body = functools.partial(_rwkv_proj_body, tiles_per_seq=seq // tm)
    return pl.pallas_call(
        body, grid=(m // tm,), in_specs=in_specs, out_specs=[_rows(tm, d)] * 8,
        out_shape=[f32o, f32o, bf16o, f32o, f32o, f32o, bf16o, f32o],
        compiler_params=_params("parallel"), name="rwkv_proj")(
        h, h, g, p["mix"], p["wr"], p["wk"], p["wv"], p["w0"], p["w1"], p["w2"], p["a0"], p["a1"], p["a2"],
        p["g1"], p["g2"], p["k_k"], p["k_a"], p["r_k"], p["e"], p["et"])


def _blk(x, blkmask):
    return jnp.concatenate([x] * WKV_GROUP, axis=0) * blkmask


def _wkv_body(lw_ref, r_ref, k_ref, v_ref, kk_ref, b_ref, blk_ref, y_ref, h_ref, *, n_sub):
    c = WKV_CHUNK
    gw = MXU_DIM

    @pl.when(pl.program_id(1) == 0)
    def _():
        h_ref[...] = jnp.zeros_like(h_ref)

    blkmask = blk_ref[...]
    row = lax.broadcasted_iota(jnp.int32, (c, gw), 0)
    colj = lax.broadcasted_iota(jnp.int32, (c, gw), 1) % c
    strict, incl = row > colj, row >= colj
    eye_w = (row == colj).astype(F32)
    eye_b = eye_w.astype(BF16)
    r2 = lax.broadcasted_iota(jnp.int32, (c, c), 0)
    c2 = lax.broadcasted_iota(jnp.int32, (c, c), 1)
    tri = (r2 >= c2).astype(BF16)

    blk = lambda x: _blk(x, blkmask)
    mm = lambda a, b: jnp.dot(a, b, preferred_element_type=F32)
    stack = lambda *a: jnp.concatenate(a, axis=0)
    side = lambda *a: jnp.concatenate(a, axis=1)
    groups = [slice(g * gw, (g + 1) * gw) for g in range(h_ref.shape[1] // gw)]

    a_t, r_f, r_t, b_t, k_t, b_c, k_c, vg, w_end = ([] for _ in range(9))
    for ci in range(n_sub):
        rows = slice(ci * c, (ci + 1) * c)
        lw = lw_ref[rows, :]
        cs = _dot_hilo_left(tri, lw)
        cs_end = cs[c - 1:c, :]
        w_inv = jnp.exp(-cs)
        w_rem = jnp.exp(cs_end - cs)
        a_all = (-kk_ref[rows, :] * jnp.exp(cs - lw)).astype(BF16)
        r_all = r_ref[rows, :] * jnp.exp(cs)
        b_raw, k_raw, v_all = b_ref[rows, :], k_ref[rows, :], v_ref[rows, :]
        bt_all, kt_all = (b_raw * w_inv).astype(BF16), (k_raw * w_inv).astype(BF16)
        bc_all, kc_all = (b_raw * w_rem).astype(BF16), (k_raw * w_rem).astype(BF16)
        we_all = jnp.exp(cs_end)
        for sl in groups:
            a_t.append(a_all[:, sl])
            r_f.append(r_all[:, sl])
            r_t.append(r_all[:, sl].astype(BF16))
            b_t.append(bt_all[:, sl])
            k_t.append(kt_all[:, sl])
            b_c.append(bc_all[:, sl])
            k_c.append(kc_all[:, sl])
            vg.append(v_all[:, sl])
            w_end.append(we_all[:, sl])

    vblk = [blk(v) for v in vg]
    ar = [stack(a, r) for a, r in zip(a_t, r_t)]
    sb = [_dot_nt(x, blk(b)) for x, b in zip(ar, b_t)]
    sk = [_dot_nt(x, blk(k)) for x, k in zip(ar, k_t)]
    bkt = [_dot_nt(eye_b, stack(blk(b), blk(k))).astype(BF16) for b, k in zip(b_c, k_c)]
    lab = [jnp.where(strict, s[:c], 0.0) for s in sb]
    mrb = [jnp.where(incl, s[c:], 0.0).astype(BF16) for s in sb]
    lmk = [stack(jnp.where(strict, s[:c], 0.0).astype(BF16), jnp.where(incl, s[c:], 0.0).astype(BF16), u[:, gw:])
           for s, u in zip(sk, bkt)]
    t = [eye_w + x for x in lab]
    x = [v.astype(BF16) for v in lab]
    x = [mm(v, blk(v)).astype(BF16) for v in x]
    for _ in range(4):
        tx = [mm(stack(tv.astype(BF16), xv), blk(xv)) for tv, xv in zip(t, x)]
        t = [tv + u[:c] for tv, u in zip(t, tx)]
        x = [u[c:].astype(BF16) for u in tx]
    t = [tv + mm(tv.astype(BF16), blk(xv)) for tv, xv in zip(t, x)]
    lmkv = [mm(l_, v) for l_, v in zip(lmk, vblk)]
    pq = [mm(tv.astype(BF16), side(blk(a), blk(u[:c].astype(BF16)))).astype(BF16)
          for tv, u, a in zip(t, lmkv, a_t)]
    ryg = [mm(stack(m_, u[:, :gw]), side(blk(w[:, :gw]), blk(w[:, gw:]))) for m_, u, w in zip(mrb, bkt, pq)]
    r_hat = [rf + u[:c, :gw] for rf, u in zip(r_f, ryg)]
    y_hat = [u[:c, gw:] + w[c:2 * c] for u, w in zip(ryg, lmkv)]
    g_w = [eye_w * we + u[c:, :gw] for we, u in zip(w_end, ryg)]
    z_w = [u[c:, gw:] + w[2 * c:] for u, w in zip(ryg, lmkv)]
    lhs = [stack(rh, gv).astype(BF16) for rh, gv in zip(r_hat, g_w)]
    state = [h_ref[:, sl] for sl in groups]
    for ci in range(n_sub):
        for gi, sl in enumerate(groups):
            i = ci * len(groups) + gi
            u = mm(lhs[i], blk(state[gi].astype(BF16)))
            y_ref[ci * c:(ci + 1) * c, sl] = u[:c] + y_hat[i]
            state[gi] = u[c:] + z_w[i]
    for gi, sl in enumerate(groups):
        h_ref[:, sl] = state[gi]


def _dot_hilo_left(a, b):
    hi = b.astype(BF16)
    lo = (b - hi.astype(F32)).astype(BF16)
    return jnp.dot(a, hi, preferred_element_type=F32) + jnp.dot(a, lo, preferred_element_type=F32)


def _wkv(lw, r, k, v, kk, b, blkmask, batch, seq):
    m, d = r.shape
    rows = WKV_CHUNK * WKV_SUBCHUNKS
    nc = seq // rows
    spec = pl.BlockSpec((rows, d), lambda bi, ci: (bi * nc + ci, 0))
    return pl.pallas_call(
        functools.partial(_wkv_body, n_sub=WKV_SUBCHUNKS), grid=(batch, nc),
        in_specs=[spec] * 6 + [_resident(blkmask.shape)], out_specs=spec,
        out_shape=jax.ShapeDtypeStruct((m, d), F32),
        scratch_shapes=[pltpu.VMEM((RWKV_HEAD, d), F32)],
        compiler_params=_params("parallel", "arbitrary"), name="wkv")(lw, r, k, v, kk, b, blkmask)


def _rwkv_out_body(y_ref, bonus_ref, gate_ref, h_ref, gnw_ref, gnb_ref, wo_ref, e_ref, et_ref, out_ref):
    e, et = e_ref[...], et_ref[...]
    y = y_ref[...]
    inv_n = 1.0 / RWKV_HEAD
    mu = _dot_hilo(_dot(y, e) * inv_n, et)
    dlt = y - mu
    var = _dot(dlt * dlt, e) * inv_n
    rstd = _dot_hilo(lax.rsqrt(var + GN_EPS), et)
    yn = dlt * rstd * gnw_ref[...] + gnb_ref[...]
    out = (yn + bonus_ref[...]) * gate_ref[...].astype(F32)
    out_ref[...] = h_ref[...] + _dot(out, wo_ref[...])


def _rwkv_out(y, bonus, gate, h, p):
    m, d = h.shape
    tm = TOKEN_TILE
    vec = _resident((1, d))
    return pl.pallas_call(
        _rwkv_out_body, grid=(m // tm,),
        in_specs=[_rows(tm, d)] * 4 + [vec, vec, _resident((d, d)), _resident(p["e"].shape), _resident(p["et"].shape)],
        out_specs=_rows(tm, d), out_shape=jax.ShapeDtypeStruct((m, d), F32),
        compiler_params=_params("parallel"), name="rwkv_out")(
        y, bonus, gate, h, p["gn_w"], p["gn_b"], p["wo"], p["e"], p["et"])


def _rope_body(pos_ref, freq_ref, sign_ref, cos_ref, sin_ref):
    ang = pos_ref[...].astype(F32) * freq_ref[...]
    cos_ref[...] = jnp.cos(ang)
    sin_ref[...] = jnp.sin(ang) * sign_ref[...]


def _rope_tables(pos, freq, sign):
    m = pos.shape[0]
    tm = TOKEN_TILE
    out = jax.ShapeDtypeStruct((m, LANES), F32)
    return pl.pallas_call(
        _rope_body, grid=(m // tm,),
        in_specs=[_rows(tm, 1), _resident((1, LANES)), _resident((1, LANES))],
        out_specs=[_rows(tm, LANES)] * 2, out_shape=[out, out],
        compiler_params=_params("parallel"), name="rope_tables")(pos, freq, sign)


def _mla_kv_body(h_ref, g_ref, wd_ref, lg_ref, wu_ref, cos_ref, sin_ref, kn_out, kr_out, v_out):
    hn = _rms(h_ref[...], g_ref[...])
    ckv = _dot(hn, wd_ref[...])
    lat = _rms(ckv[:, :KV_LORA], lg_ref[...])
    kv = _dot(lat, wu_ref[...])
    half = kv.shape[1] // 2
    kn_out[...] = kv[:, :half].astype(BF16)
    v_out[...] = kv[:, half:].astype(BF16)
    kr = ckv[:, KV_LORA:KV_LORA + LANES]
    krs = ckv[:, KV_LORA + LANES:]
    kr_out[...] = (kr * cos_ref[...] + krs * sin_ref[...]).astype(BF16)


def _mla_kv(h, p, cos, sin):
    m, d = h.shape
    tm = TOKEN_TILE
    hv = MLA_HEADS * V_HEAD
    return pl.pallas_call(
        _mla_kv_body, grid=(m // tm,),
        in_specs=[_rows(tm, d), _resident((1, d)), _resident(p["w_dkv"].shape), _resident((1, KV_LORA)),
                  _resident(p["w_ukv"].shape), _rows(tm, LANES), _rows(tm, LANES)],
        out_specs=[_rows(tm, hv), _rows(tm, LANES), _rows(tm, hv)],
        out_shape=[jax.ShapeDtypeStruct((m, hv), BF16), jax.ShapeDtypeStruct((m, LANES), BF16),
                   jax.ShapeDtypeStruct((m, hv), BF16)],
        compiler_params=_params("parallel"), name="mla_kv")(
        h, p["kv_norm_g"], p["w_dkv"], p["kv_latent_g"], p["w_ukv"], cos, sin)


def _mla_q_body(h_ref, g_ref, wd_ref, lg_ref, wu_ref, cos_ref, sin_ref, qn_out, qr_out, *, scale):
    hn = _rms(h_ref[...], g_ref[...])
    lat = _rms(_dot(hn, wd_ref[...]), lg_ref[...])
    q = _dot(lat, wu_ref[...]) * scale
    w = q.shape[1] // 3
    cos = jnp.concatenate([cos_ref[...]] * MLA_HEADS, axis=1)
    sin = jnp.concatenate([sin_ref[...]] * MLA_HEADS, axis=1)
    qn_out[...] = q[:, :w].astype(BF16)
    qr_out[...] = (q[:, w:2 * w] * cos + q[:, 2 * w:] * sin).astype(BF16)


def _mla_q(h, g, p, cos, sin):
    m, d = h.shape
    tm = TOKEN_TILE
    w = MLA_HEADS * LANES
    scale = float(QK_NOPE + QK_ROPE) ** -0.5 * float(np.log2(np.e))
    out = jax.ShapeDtypeStruct((m, w), BF16)
    return pl.pallas_call(
        functools.partial(_mla_q_body, scale=scale), grid=(m // tm,),
        in_specs=[_rows(tm, d), _resident((1, d)), _resident(p["w_dq"].shape), _resident(p["q_latent_g"].shape),
                  _resident(p["w_uq"].shape), _rows(tm, LANES), _rows(tm, LANES)],
        out_specs=[_rows(tm, w)] * 2, out_shape=[out, out],
        compiler_params=_params("parallel"), name="mla_q")(
        h, g, p["w_dq"], p["q_latent_g"], p["w_uq"], cos, sin)


def _attn_body(qn_ref, qr_ref, kn_ref, kr_ref, v_ref, o_ref):
    t = ATTN_TILE
    rows = ATTN_CHAIN
    qi = pl.program_id(2)
    chains = [jnp.concatenate([qn_ref[i * rows:(i + 1) * rows, :], qr_ref[i * rows:(i + 1) * rows, :]], axis=1)
              for i in range(t // rows)]

    def keys(off, n):
        return jnp.concatenate([kn_ref[pl.ds(off, n), :], kr_ref[pl.ds(off, n), :]], axis=1)

    def update(carry, s, v):
        m_i, l_i, acc = carry
        m_new = jnp.maximum(m_i, jnp.max(s, axis=-1, keepdims=True))
        p = jnp.exp2(s - m_new)
        alpha = jnp.exp2(m_i - m_new)
        l_new = alpha * l_i + jnp.sum(p, axis=-1, keepdims=True)
        return m_new, l_new, alpha * acc + jnp.dot(p.astype(BF16), v, preferred_element_type=F32)

    def full_tile(j, carry):
        off = pl.multiple_of(j * t, t)
        k = keys(off, t)
        v = v_ref[pl.ds(off, t), :]
        s = [_dot_nt(x, k) for x in chains]
        return tuple(update(c_, s_, v) for c_, s_ in zip(carry, s))

    one = (jnp.full((rows, 1), -jnp.inf, F32), jnp.zeros((rows, 1), F32), jnp.zeros((rows, V_HEAD), F32))
    carry = lax.fori_loop(0, qi, full_tile, (one,) * len(chains))
    off = pl.multiple_of(qi * t, t)
    outs = []
    s_diag = []
    for i, x in enumerate(chains):
        n = (i + 1) * rows
        mask = (lax.broadcasted_iota(jnp.int32, (rows, n), 0) + i * rows
                >= lax.broadcasted_iota(jnp.int32, (rows, n), 1))
        s_diag.append(jnp.where(mask, _dot_nt(x, keys(off, n)), -1e30))
    for i, s in enumerate(s_diag):
        _, l_i, acc = update(carry[i], s, v_ref[pl.ds(off, (i + 1) * rows), :])
        outs.append(acc / l_i)
    o_ref[...] = jnp.concatenate(outs, axis=0).astype(BF16)


def _attention(qn, qr, kn, kr, v, batch, seq):
    m = qn.shape[0]
    t = ATTN_TILE
    nq = seq // t
    qspec = pl.BlockSpec((t, LANES), lambda b, h, i: (b * nq + i, h))
    kspec = pl.BlockSpec((seq, LANES), lambda b, h, i: (b, h))
    krspec = pl.BlockSpec((seq, LANES), lambda b, h, i: (b, 0))
    return pl.pallas_call(
        _attn_body, grid=(batch, MLA_HEADS, nq),
        in_specs=[qspec, qspec, kspec, krspec, kspec], out_specs=qspec,
        out_shape=jax.ShapeDtypeStruct((m, MLA_HEADS * V_HEAD), BF16),
        compiler_params=_params("parallel", "parallel", "arbitrary"), name="mla_attention")(qn, qr, kn, kr, v)


def _prep_rwkv(mix, w_r, w_k, w_v, w_o, w0, w1, w2, a0, a1, a2, g1, g2, k_k, k_a, r_k, gn_w, gn_b):
    d = w_r.shape[0]
    pad_c = lambda w: jnp.pad(w, ((0, 0), (0, LANES - w.shape[1]))).astype(BF16)
    pad_r = lambda w: jnp.pad(w, ((0, LANES - w.shape[0]), (0, 0))).astype(BF16)
    head_of = np.arange(d) // RWKV_HEAD
    e = (head_of[:, None] == np.arange(LANES)[None, :]).astype(np.float32)
    return dict(
        mix=mix, wr=w_r.astype(BF16), wk=w_k.astype(BF16), wv=w_v.astype(BF16), wo=w_o.astype(BF16),
        w0=w0[None, :], w1=pad_c(w1), w2=pad_r(w2), a0=a0[None, :], a1=pad_c(a1), a2=pad_r(a2),
        g1=g1.astype(BF16), g2=g2.astype(BF16), k_k=k_k[None, :], k_a=k_a[None, :], r_k=r_k.reshape(1, d),
        gn_w=gn_w[None, :], gn_b=gn_b[None, :], e=jnp.asarray(e, BF16), et=jnp.asarray(e.T, BF16))


def _prep_mla(kv_norm_g, w_dkv, kv_latent_g, w_ukv, w_dq, q_latent_g, w_uq, w_o):
    half = QK_ROPE // 2
    zpad = LANES - QK_ROPE
    x1 = w_dkv[:, KV_LORA:KV_LORA + half]
    x2 = w_dkv[:, KV_LORA + half:]
    z = jnp.zeros((w_dkv.shape[0], zpad), w_dkv.dtype)
    w_dkv_p = jnp.concatenate([w_dkv[:, :KV_LORA], x1, x2, z, x2, x1, z], axis=1)
    ukv = w_ukv.reshape(KV_LORA, MLA_HEADS, QK_NOPE + V_HEAD)
    w_ukv_p = jnp.concatenate([ukv[:, :, :QK_NOPE].reshape(KV_LORA, -1), ukv[:, :, QK_NOPE:].reshape(KV_LORA, -1)], axis=1)
    uq = w_uq.reshape(w_uq.shape[0], MLA_HEADS, QK_NOPE + QK_ROPE)
    q1 = uq[:, :, QK_NOPE:QK_NOPE + half]
    q2 = uq[:, :, QK_NOPE + half:]
    zq = jnp.zeros(q1.shape[:2] + (zpad,), w_uq.dtype)
    flat = lambda t: t.reshape(t.shape[0], -1)
    w_uq_p = jnp.concatenate([flat(uq[:, :, :QK_NOPE]), flat(jnp.concatenate([q1, q2, zq], axis=2)),
                              flat(jnp.concatenate([q2, q1, zq], axis=2))], axis=1)
    return dict(kv_norm_g=kv_norm_g[None, :], w_dkv=w_dkv_p.astype(BF16), kv_latent_g=kv_latent_g[None, :],
                w_ukv=w_ukv_p.astype(BF16), w_dq=w_dq.astype(BF16), q_latent_g=q_latent_g[None, :],
                w_uq=w_uq_p.astype(BF16), w_o=w_o.astype(BF16))


def _rope_consts():
    half = QK_ROPE // 2
    inv_freq = ROPE_THETA ** (-jnp.arange(0, QK_ROPE, 2, dtype=F32) / QK_ROPE)
    reps = LANES // half
    freq = jnp.tile(inv_freq, reps)[None, :]
    sign = jnp.tile(jnp.concatenate([-jnp.ones((half,), F32), jnp.ones((half,), F32)]), reps // 2)[None, :]
    return freq, sign


def kernel(x, positions, norm_g, ffn_w_gate, ffn_w_up, ffn_w_down, rwkv_mix, rwkv_w_r, rwkv_w_k, rwkv_w_v, rwkv_w_o, rwkv_w0, rwkv_w1, rwkv_w2, rwkv_a0, rwkv_a1, rwkv_a2, rwkv_g1, rwkv_g2, rwkv_k_k, rwkv_k_a, rwkv_r_k, rwkv_gn_w, rwkv_gn_b, kv_norm_g, mla_w_dkv, mla_kv_latent_g, mla_w_ukv, mla_w_dq, mla_q_latent_g, mla_w_uq, mla_w_o, final_norm_g):
    batch, seq, d = x.shape
    depth = norm_g.shape[0]
    n_a = rwkv_mix.shape[0]
    m = batch * seq
    assert seq % TOKEN_TILE == 0 and seq % ATTN_TILE == 0 and seq % WKV_CHUNK == 0 and d % MXU_DIM == 0

    wg, wu, wd = ffn_w_gate.astype(BF16), ffn_w_up.astype(BF16), ffn_w_down.astype(BF16)
    blkmask = jnp.asarray(np.kron(np.eye(WKV_GROUP), np.ones((RWKV_HEAD, RWKV_HEAD))), BF16)
    freq, sign = _rope_consts()
    cos, sin = _rope_tables(positions.reshape(m, 1), freq, sign)
    mla = None

    h = x.reshape(m, d)
    kv = None
    for layer in range(depth):
        norm = lambda j: norm_g[layer, j][None, :]
        if layer == n_a:
            mla = _prep_mla(kv_norm_g, mla_w_dkv, mla_kv_latent_g, mla_w_ukv, mla_w_dq[0], mla_q_latent_g[0],
                            mla_w_uq[0], mla_w_o[0])
            kv = _mla_kv(h, mla, cos, sin)
        h = _ffn(h, norm(0), wg[layer, 0], wu[layer, 0], wd[layer, 0])
        last = layer == depth - 1
        if layer < n_a:
            i = layer
            p = _prep_rwkv(rwkv_mix[i], rwkv_w_r[i], rwkv_w_k[i], rwkv_w_v[i], rwkv_w_o[i], rwkv_w0[i], rwkv_w1[i],
                           rwkv_w2[i], rwkv_a0[i], rwkv_a1[i], rwkv_a2[i], rwkv_g1[i], rwkv_g2[i], rwkv_k_k[i],
                           rwkv_k_a[i], rwkv_r_k[i], rwkv_gn_w[i], rwkv_gn_b[i])
            r, k, v, kk, b, lw, gate, bonus = _rwkv_proj(h, norm(1), p, seq)
            y = _wkv(lw, r, k, v, kk, b, blkmask, batch, seq)
            h = _rwkv_out(y, bonus, gate, h, p)
            pre = None
        else:
            j = layer - n_a
            if j > 0:
                mla = dict(mla, **{n: v_ for n, v_ in _prep_mla(
                    kv_norm_g, mla_w_dkv, mla_kv_latent_g, mla_w_ukv, mla_w_dq[j], mla_q_latent_g[j], mla_w_uq[j],
                    mla_w_o[j]).items() if n in ("w_dq", "q_latent_g", "w_uq", "w_o")})
            qn, qr = _mla_q(h, norm(1), mla, cos, sin)
            o = _attention(qn, qr, kv[0], kv[1], kv[2], batch, seq)
            pre = (o, mla["w_o"])
        h = _ffn(h, norm(2), wg[layer, 1], wu[layer, 1], wd[layer, 1], pre=pre,
                 final_g=final_norm_g[None, :] if last else None)
    return h.reshape(batch, seq, d)
```

```python
import functools

import numpy as np
import jax
import jax.numpy as jnp
from jax import lax
from jax.experimental import pallas as pl
from jax.experimental.pallas import tpu as pltpu

F32, BF16 = jnp.float32, jnp.bfloat16

RWKV_HEAD = 64
GN_EPS = 64e-5
RMS_EPS = 1e-6
MLA_HEADS = 8
QK_NOPE = 128
QK_ROPE = 64
V_HEAD = 128
KV_LORA = 256
ROPE_THETA = 10000.0

LANES = 128
MXU_DIM = 256
VMEM_LIMIT = 48 * 1024 * 1024

TOKEN_TILE = 512
WKV_CHUNK = 64
WKV_GROUP = MXU_DIM // RWKV_HEAD
WKV_SUBCHUNKS = 4
ATTN_TILE = 1024
ATTN_CHAIN = 256


def _params(*sem):
    return pltpu.CompilerParams(dimension_semantics=sem, vmem_limit_bytes=VMEM_LIMIT)


def _resident(shape):
    return pl.BlockSpec(shape, lambda *_: (0,) * len(shape), pipeline_mode=pl.Buffered(1))


def _rows(tm, d):
    return pl.BlockSpec((tm, d), lambda i: (i, 0))


def _dot(a, b):
    return jnp.dot(a.astype(BF16), b.astype(BF16), preferred_element_type=F32)


def _dot_nt(a, b):
    return lax.dot_general(a.astype(BF16), b.astype(BF16), (((1,), (1,)), ((), ())),
                           preferred_element_type=F32)


def _dot_hilo(a, b):
    hi = a.astype(BF16)
    lo = (a - hi.astype(F32)).astype(BF16)
    return (jnp.dot(hi, b, preferred_element_type=F32) + jnp.dot(lo, b, preferred_element_type=F32))


def _rms(x, g):
    return x * lax.rsqrt(jnp.mean(x * x, axis=-1, keepdims=True) + RMS_EPS) * g


def _ffn_body(*refs, n_chunks, pre_proj, final):
    refs = list(refs)
    x_ref = refs.pop(0)
    x = x_ref[...]
    if pre_proj:
        o_ref_in, wo_ref = refs.pop(0), refs.pop(0)
        x = x + jnp.dot(o_ref_in[...], wo_ref[...], preferred_element_type=F32)
    g_ref, wg_ref, wu_ref, wd_ref = refs[:4]
    out_ref = refs[-1]
    xn = _rms(x, g_ref[...]).astype(BF16)
    tiles = wg_ref.shape[1] // MXU_DIM
    bounds = [MXU_DIM * (tiles * c // n_chunks) for c in range(n_chunks + 1)]
    chunks = [slice(lo, hi) for lo, hi in zip(bounds, bounds[1:])]
    gu = [(jnp.dot(xn, wg_ref[:, sl], preferred_element_type=F32),
           jnp.dot(xn, wu_ref[:, sl], preferred_element_type=F32)) for sl in chunks]
    acts = [(gate * jax.nn.sigmoid(gate) * up).astype(BF16) for gate, up in gu]
    acc = sum(jnp.dot(a, wd_ref[sl, :], preferred_element_type=F32) for a, sl in zip(acts, chunks))
    y = x + 0.5 * acc
    if final:
        y = _rms(y, refs[4][...])
    out_ref[...] = y


def _ffn(x, g, wg, wu, wd, *, pre=None, final_g=None):
    m, d = x.shape
    f = wg.shape[1]
    tm = TOKEN_TILE
    ins, specs = [x], [_rows(tm, d)]
    if pre is not None:
        o, wo = pre
        ins += [o, wo]
        specs += [_rows(tm, o.shape[1]), _resident(wo.shape)]
    ins += [g, wg, wu, wd]
    specs += [_resident((1, d)), _resident((d, f)), _resident((d, f)), _resident((f, d))]
    if final_g is not None:
        ins.append(final_g)
        specs.append(_resident((1, d)))
    body = functools.partial(_ffn_body, n_chunks=3, pre_proj=pre is not None, final=final_g is not None)
    return pl.pallas_call(
        body, grid=(m // tm,), in_specs=specs, out_specs=_rows(tm, d),
        out_shape=jax.ShapeDtypeStruct((m, d), F32), compiler_params=_params("parallel"),
        name="ffn")(*ins)


def _softplus(z):
    return jnp.maximum(z, 0.0) + jnp.log(1.0 + jnp.exp(-jnp.abs(z)))


def _rwkv_proj_body(h_ref, hp_ref, g_ref, mix_ref, wr_ref, wk_ref, wv_ref, w0_ref, w1_ref, w2_ref,
                    a0_ref, a1_ref, a2_ref, g1_ref, g2_ref, kk_ref, ka_ref, rk_ref, e_ref, et_ref,
                    r_out, k_out, v_out, kkn_out, b_out, lw_out, gate_out, bonus_out, *, tiles_per_seq):
    i = pl.program_id(0)
    gn = g_ref[...]
    hn = _rms(h_ref[...], gn)
    tm = hn.shape[0]
    prev = _rms(hp_ref[...], gn)[7:8, :]
    prev = jnp.where(i % tiles_per_seq == 0, 0.0, prev)
    shifted = pltpu.roll(hn, 1, 0)
    row = lax.broadcasted_iota(jnp.int32, (tm, 1), 0)
    shifted = jnp.where(row == 0, prev, shifted)
    xx = shifted - hn
    xr, xw, xk, xv, xa, xg = (hn + xx * mix_ref[c:c + 1, :] for c in range(6))
    r = _dot(xr, wr_ref[...])
    k = _dot(xk, wk_ref[...])
    v = _dot(xv, wv_ref[...])
    wl = w0_ref[...] + _dot(jnp.tanh(_dot(xw, w1_ref[...])), w2_ref[...])
    w_log = -_softplus(-wl) - 0.5
    lw_out[...] = -jnp.exp(w_log)
    a = jax.nn.sigmoid(a0_ref[...] + _dot(_dot(xa, a1_ref[...]), a2_ref[...]))
    gate_out[...] = _dot(jax.nn.sigmoid(_dot(xg, g1_ref[...])), g2_ref[...]).astype(BF16)
    e, et = e_ref[...], et_ref[...]
    kk = k * kk_ref[...]
    inv = lax.rsqrt(jnp.maximum(_dot(kk * kk, e), 1e-24))
    kk = kk * _dot_hilo(inv, et)
    k = k * (1.0 + (a - 1.0) * ka_ref[...])
    r_out[...] = r
    k_out[...] = k
    v_out[...] = v.astype(BF16)
    kkn_out[...] = kk
    b_out[...] = kk * a
    bonus_out[...] = _dot(_dot(r * k * rk_ref[...], e), et) * v


def _rwkv_proj(h, g, p, seq):
    m, d = h.shape
    tm = TOKEN_TILE
    lora_specs = [_resident(p[n].shape) for n in ("w0", "w1", "w2", "a0", "a1", "a2", "g1", "g2")]
    vec = _resident((1, d))
    in_specs = ([_rows(tm, d), pl.BlockSpec((8, d), lambda i: (jnp.maximum(i * (tm // 8) - 1, 0), 0)),
                 vec, _resident((6, d)), _resident((d, d)), _resident((d, d)), _resident((d, d))]
                + lora_specs + [vec, vec, vec, _resident(p["e"].shape), _resident(p["et"].shape)])
    f32o = jax.ShapeDtypeStruct((m, d), F32)
    bf16o = jax.ShapeDtypeStruct((m, d), BF16)
    body = functools.partial(_rwkv_proj_body, tiles_per_seq=seq // tm)
    return pl.pallas_call(
        body, grid=(m // tm,), in_specs=in_specs, out_specs=[_rows(tm, d)] * 8,
        out_shape=[f32o, f32o, bf16o, f32o, f32o, f32o, bf16o, f32o],
        compiler_params=_params("parallel"), name="rwkv_proj")(
        h, h, g, p["mix"], p["wr"], p["wk"], p["wv"], p["w0"], p["w1"], p["w2"], p["a0"], p["a1"], p["a2"],
        p["g1"], p["g2"], p["k_k"], p["k_a"], p["r_k"], p["e"], p["et"])


def _blk(x, blkmask):
    return jnp.concatenate([x] * WKV_GROUP, axis=0) * blkmask


def _wkv_body(lw_ref, r_ref, k_ref, v_ref, kk_ref, b_ref, blk_ref, y_ref, h_ref, *, n_sub):
    c = WKV_CHUNK
    gw = MXU_DIM

    @pl.when(pl.program_id(1) == 0)
    def _():
        h_ref[...] = jnp.zeros_like(h_ref)

    blkmask = blk_ref[...]
    row = lax.broadcasted_iota(jnp.int32, (c, gw), 0)
    colj = lax.broadcasted_iota(jnp.int32, (c, gw), 1) % c
    strict, incl = row > colj, row >= colj
    eye_w = (row == colj).astype(F32)
    eye_b = eye_w.astype(BF16)
    r2 = lax.broadcasted_iota(jnp.int32, (c, c), 0)
    c2 = lax.broadcasted_iota(jnp.int32, (c, c), 1)
    tri = (r2 >= c2).astype(BF16)

    blk = lambda x: _blk(x, blkmask)
    mm = lambda a, b: jnp.dot(a, b, preferred_element_type=F32)
    stack = lambda *a: jnp.concatenate(a, axis=0)
    side = lambda *a: jnp.concatenate(a, axis=1)
    groups = [slice(g * gw, (g + 1) * gw) for g in range(h_ref.shape[1] // gw)]

    a_t, r_f, r_t, b_t, k_t, b_c, k_c, vg, w_end = ([] for _ in range(9))
    for ci in range(n_sub):
        rows = slice(ci * c, (ci + 1) * c)
        lw = lw_ref[rows, :]
        cs = _dot_hilo_left(tri, lw)
        cs_end = cs[c - 1:c, :]
        w_inv = jnp.exp(-cs)
        w_rem = jnp.exp(cs_end - cs)
        a_all = (-kk_ref[rows, :] * jnp.exp(cs - lw)).astype(BF16)
        r_all = r_ref[rows, :] * jnp.exp(cs)
        b_raw, k_raw, v_all = b_ref[rows, :], k_ref[rows, :], v_ref[rows, :]
        bt_all, kt_all = (b_raw * w_inv).astype(BF16), (k_raw * w_inv).astype(BF16)
        bc_all, kc_all = (b_raw * w_rem).astype(BF16), (k_raw * w_rem).astype(BF16)
        we_all = jnp.exp(cs_end)
        for sl in groups:
            a_t.append(a_all[:, sl])
            r_f.append(r_all[:, sl])
            r_t.append(r_all[:, sl].astype(BF16))
            b_t.append(bt_all[:, sl])
            k_t.append(kt_all[:, sl])
            b_c.append(bc_all[:, sl])
            k_c.append(kc_all[:, sl])
            vg.append(v_all[:, sl])
            w_end.append(we_all[:, sl])

    vblk = [blk(v) for v in vg]
    ar = [stack(a, r) for a, r in zip(a_t, r_t)]
    sb = [_dot_nt(x, blk(b)) for x, b in zip(ar, b_t)]
    sk = [_dot_nt(x, blk(k)) for x, k in zip(ar, k_t)]
    bkt = [_dot_nt(eye_b, stack(blk(b), blk(k))).astype(BF16) for b, k in zip(b_c, k_c)]
    lab = [jnp.where(strict, s[:c], 0.0) for s in sb]
    mrb = [jnp.where(incl, s[c:], 0.0).astype(BF16) for s in sb]
    lmk = [stack(jnp.where(strict, s[:c], 0.0).astype(BF16), jnp.where(incl, s[c:], 0.0).astype(BF16), u[:, gw:])
           for s, u in zip(sk, bkt)]
    t = [eye_w + x for x in lab]
    x = [v.astype(BF16) for v in lab]
    x = [mm(v, blk(v)).astype(BF16) for v in x]
    for _ in range(4):
        tx = [mm(stack(tv.astype(BF16), xv), blk(xv)) for tv, xv in zip(t, x)]
        t = [tv + u[:c] for tv, u in zip(t, tx)]
        x = [u[c:].astype(BF16) for u in tx]
    t = [tv + mm(tv.astype(BF16), blk(xv)) for tv, xv in zip(t, x)]
    lmkv = [mm(l_, v) for l_, v in zip(lmk, vblk)]
    pq = [mm(tv.astype(BF16), side(blk(a), blk(u[:c].astype(BF16)))).astype(BF16)
          for tv, u, a in zip(t, lmkv, a_t)]
    ryg = [mm(stack(m_, u[:, :gw]), side(blk(w[:, :gw]), blk(w[:, gw:]))) for m_, u, w in zip(mrb, bkt, pq)]
    r_hat = [rf + u[:c, :gw] for rf, u in zip(r_f, ryg)]
    y_hat = [u[:c, gw:] + w[c:2 * c] for u, w in zip(ryg, lmkv)]
    g_w = [eye_w * we + u[c:, :gw] for we, u in zip(w_end, ryg)]
    z_w = [u[c:, gw:] + w[2 * c:] for u, w in zip(ryg, lmkv)]
    lhs = [stack(rh, gv).astype(BF16) for rh, gv in zip(r_hat, g_w)]
    state = [h_ref[:, sl] for sl in groups]
    for ci in range(n_sub):
        for gi, sl in enumerate(groups):
            i = ci * len(groups) + gi
            u = mm(lhs[i], blk(state[gi].astype(BF16)))
            y_ref[ci * c:(ci + 1) * c, sl] = u[:c] + y_hat[i]
            state[gi] = u[c:] + z_w[i]
    for gi, sl in enumerate(groups):
        h_ref[:, sl] = state[gi]


def _dot_hilo_left(a, b):
    hi = b.astype(BF16)
    lo = (b - hi.astype(F32)).astype(BF16)
    return jnp.dot(a, hi, preferred_element_type=F32) + jnp.dot(a, lo, preferred_element_type=F32)


def _wkv(lw, r, k, v, kk, b, blkmask, batch, seq):
    m, d = r.shape
    rows = WKV_CHUNK * WKV_SUBCHUNKS
    nc = seq // rows
    spec = pl.BlockSpec((rows, d), lambda bi, ci: (bi * nc + ci, 0))
    return pl.pallas_call(
        functools.partial(_wkv_body, n_sub=WKV_SUBCHUNKS), grid=(batch, nc),
        in_specs=[spec] * 6 + [_resident(blkmask.shape)], out_specs=spec,
        out_shape=jax.ShapeDtypeStruct((m, d), F32),
        scratch_shapes=[pltpu.VMEM((RWKV_HEAD, d), F32)],
        compiler_params=_params("parallel", "arbitrary"), name="wkv")(lw, r, k, v, kk, b, blkmask)


def _rwkv_out_body(y_ref, bonus_ref, gate_ref, h_ref, gnw_ref, gnb_ref, wo_ref, e_ref, et_ref, out_ref):
    e, et = e_ref[...], et_ref[...]
    y = y_ref[...]
    inv_n = 1.0 / RWKV_HEAD
    mu = _dot_hilo(_dot(y, e) * inv_n, et)
    dlt = y - mu
    var = _dot(dlt * dlt, e) * inv_n
    rstd = _dot_hilo(lax.rsqrt(var + GN_EPS), et)
    yn = dlt * rstd * gnw_ref[...] + gnb_ref[...]
    out = (yn + bonus_ref[...]) * gate_ref[...].astype(F32)
    out_ref[...] = h_ref[...] + _dot(out, wo_ref[...])


def _rwkv_out(y, bonus, gate, h, p):
    m, d = h.shape
    tm = TOKEN_TILE
    vec = _resident((1, d))
    return pl.pallas_call(
        _rwkv_out_body, grid=(m // tm,),
        in_specs=[_rows(tm, d)] * 4 + [vec, vec, _resident((d, d)), _resident(p["e"].shape), _resident(p["et"].shape)],
        out_specs=_rows(tm, d), out_shape=jax.ShapeDtypeStruct((m, d), F32),
        compiler_params=_params("parallel"), name="rwkv_out")(
        y, bonus, gate, h, p["gn_w"], p["gn_b"], p["wo"], p["e"], p["et"])


def _rope_body(pos_ref, freq_ref, sign_ref, cos_ref, sin_ref):
    ang = pos_ref[...].astype(F32) * freq_ref[...]
    cos_ref[...] = jnp.cos(ang)
    sin_ref[...] = jnp.sin(ang) * sign_ref[...]


def _rope_tables(pos, freq, sign):
    m = pos.shape[0]
    tm = TOKEN_TILE
    out = jax.ShapeDtypeStruct((m, LANES), F32)
    return pl.pallas_call(
        _rope_body, grid=(m // tm,),
        in_specs=[_rows(tm, 1), _resident((1, LANES)), _resident((1, LANES))],
        out_specs=[_rows(tm, LANES)] * 2, out_shape=[out, out],
        compiler_params=_params("parallel"), name="rope_tables")(pos, freq, sign)


def _mla_kv_body(h_ref, g_ref, wd_ref, lg_ref, wuk_ref, wuvt_ref, cos_ref, sin_ref, kn_out, kr_out, vt_out):
    hn = _rms(h_ref[...], g_ref[...])
    ckv = _dot(hn, wd_ref[...])
    lat = _rms(ckv[:, :KV_LORA], lg_ref[...]).astype(BF16)
    kn_out[...] = jnp.dot(lat, wuk_ref[...], preferred_element_type=F32).astype(BF16)
    vt_out[...] = _dot_nt(wuvt_ref[...], lat).astype(BF16)
    kr = ckv[:, KV_LORA:KV_LORA + LANES]
    krs = ckv[:, KV_LORA + LANES:]
    kr_out[...] = (kr * cos_ref[...] + krs * sin_ref[...]).astype(BF16)


def _mla_kv(h, p, cos, sin):
    m, d = h.shape
    tm = TOKEN_TILE
    hv = MLA_HEADS * V_HEAD
    return pl.pallas_call(
        _mla_kv_body, grid=(m // tm,),
        in_specs=[_rows(tm, d), _resident((1, d)), _resident(p["w_dkv"].shape), _resident((1, KV_LORA)),
                  _resident(p["w_uk"].shape), _resident(p["w_uv_t"].shape), _rows(tm, LANES), _rows(tm, LANES)],
        out_specs=[_rows(tm, hv), _rows(tm, LANES), pl.BlockSpec((hv, tm), lambda i: (0, i))],
        out_shape=[jax.ShapeDtypeStruct((m, hv), BF16), jax.ShapeDtypeStruct((m, LANES), BF16),
                   jax.ShapeDtypeStruct((hv, m), BF16)],
        compiler_params=_params("parallel"), name="mla_kv")(
        h, p["kv_norm_g"], p["w_dkv"], p["kv_latent_g"], p["w_uk"], p["w_uv_t"], cos, sin)


def _mla_q_body(h_ref, g_ref, wd_ref, lg_ref, wu_ref, cos_ref, sin_ref, qn_out, qr_out, *, scale):
    hn = _rms(h_ref[...], g_ref[...])
    lat = _rms(_dot(hn, wd_ref[...]), lg_ref[...])
    q = _dot(lat, wu_ref[...]) * scale
    w = q.shape[1] // 3
    cos = jnp.concatenate([cos_ref[...]] * MLA_HEADS, axis=1)
    sin = jnp.concatenate([sin_ref[...]] * MLA_HEADS, axis=1)
    qn_out[...] = q[:, :w].astype(BF16)
    qr_out[...] = (q[:, w:2 * w] * cos + q[:, 2 * w:] * sin).astype(BF16)


def _mla_q(h, g, p, cos, sin):
    m, d = h.shape
    tm = TOKEN_TILE
    w = MLA_HEADS * LANES
    scale = float(QK_NOPE + QK_ROPE) ** -0.5 * float(np.log2(np.e))
    out = jax.ShapeDtypeStruct((m, w), BF16)
    return pl.pallas_call(
        functools.partial(_mla_q_body, scale=scale), grid=(m // tm,),
        in_specs=[_rows(tm, d), _resident((1, d)), _resident(p["w_dq"].shape), _resident(p["q_latent_g"].shape),
                  _resident(p["w_uq"].shape), _rows(tm, LANES), _rows(tm, LANES)],
        out_specs=[_rows(tm, w)] * 2, out_shape=[out, out],
        compiler_params=_params("parallel"), name="mla_q")(
        h, g, p["w_dq"], p["q_latent_g"], p["w_uq"], cos, sin)


def _attn_body(qn_ref, qr_ref, kn_ref, kr_ref, vt_ref, o_ref):
    t = ATTN_TILE
    rows = ATTN_CHAIN
    n_chain = t // rows
    n_tiles = qn_ref.shape[0] // t

    def queries(qt, i):
        lo = qt * t + i * rows
        return jnp.concatenate([qn_ref[lo:lo + rows, :], qr_ref[lo:lo + rows, :]], axis=1)

    def keys(kt, n):
        return jnp.concatenate([kn_ref[kt * t:kt * t + n, :], kr_ref[kt * t:kt * t + n, :]], axis=1)

    def n_keys(qt, kt, i):
        return t if kt < qt else (i + 1) * rows

    def scores(qt, kt):
        out = []
        for i in range(n_chain):
            n = n_keys(qt, kt, i)
            s_t = _dot_nt(keys(kt, n), queries(qt, i))
            if kt == qt:
                mask = (lax.broadcasted_iota(jnp.int32, (n, rows), 1) + i * rows
                        >= lax.broadcasted_iota(jnp.int32, (n, rows), 0))
                s_t = jnp.where(mask, s_t, -1e30)
            out.append(s_t)
        return out

    def update(carry, s_t, v_t):
        m_i, l_i, acc = carry
        m_new = jnp.maximum(m_i, jnp.max(s_t, axis=0, keepdims=True))
        p_t = jnp.exp2(s_t - m_new)
        alpha = jnp.exp2(m_i - m_new)
        l_new = alpha * l_i + jnp.sum(p_t, axis=0, keepdims=True)
        return m_new, l_new, alpha * acc + jnp.dot(v_t, p_t.astype(BF16), preferred_element_type=F32)

    steps = [(qt, kt) for qt in range(n_tiles) for kt in range(qt + 1)]
    one = (jnp.full((1, rows), -jnp.inf, F32), jnp.zeros((1, rows), F32), jnp.zeros((V_HEAD, rows), F32))
    state = [[one] * n_chain for _ in range(n_tiles)]
    s_next = scores(*steps[0])
    for n, (qt, kt) in enumerate(steps):
        s_cur = s_next
        if n + 1 < len(steps):
            s_next = scores(*steps[n + 1])
        for i in range(n_chain):
            nk = n_keys(qt, kt, i)
            state[qt][i] = update(state[qt][i], s_cur[i], vt_ref[:, kt * t:kt * t + nk])
            if kt == qt:
                _, l_i, acc = state[qt][i]
                lo = qt * t + i * rows
                o_ref[lo:lo + rows, :] = (acc / l_i).T.astype(BF16)


def _attention(qn, qr, kn, kr, vt, batch, seq):
    m = qn.shape[0]
    spec = pl.BlockSpec((seq, LANES), lambda b, h: (b, h))
    krspec = pl.BlockSpec((seq, LANES), lambda b, h: (b, 0))
    vtspec = pl.BlockSpec((V_HEAD, seq), lambda b, h: (h, b))
    return pl.pallas_call(
        _attn_body, grid=(batch, MLA_HEADS),
        in_specs=[spec, spec, spec, krspec, vtspec], out_specs=spec,
        out_shape=jax.ShapeDtypeStruct((m, MLA_HEADS * V_HEAD), BF16),
        compiler_params=_params("parallel", "parallel"), name="mla_attention")(qn, qr, kn, kr, vt)


def _prep_rwkv(mix, w_r, w_k, w_v, w_o, w0, w1, w2, a0, a1, a2, g1, g2, k_k, k_a, r_k, gn_w, gn_b):
    d = w_r.shape[0]
    pad_c = lambda w: jnp.pad(w, ((0, 0), (0, LANES - w.shape[1]))).astype(BF16)
    pad_r = lambda w: jnp.pad(w, ((0, LANES - w.shape[0]), (0, 0))).astype(BF16)
    head_of = np.arange(d) // RWKV_HEAD
    e = (head_of[:, None] == np.arange(LANES)[None, :]).astype(np.float32)
    return dict(
        mix=mix, wr=w_r.astype(BF16), wk=w_k.astype(BF16), wv=w_v.astype(BF16), wo=w_o.astype(BF16),
        w0=w0[None, :], w1=pad_c(w1), w2=pad_r(w2), a0=a0[None, :], a1=pad_c(a1), a2=pad_r(a2),
        g1=g1.astype(BF16), g2=g2.astype(BF16), k_k=k_k[None, :], k_a=k_a[None, :], r_k=r_k.reshape(1, d),
        gn_w=gn_w[None, :], gn_b=gn_b[None, :], e=jnp.asarray(e, BF16), et=jnp.asarray(e.T, BF16))


def _prep_mla(kv_norm_g, w_dkv, kv_latent_g, w_ukv, w_dq, q_latent_g, w_uq, w_o):
    half = QK_ROPE // 2
    zpad = LANES - QK_ROPE
    x1 = w_dkv[:, KV_LORA:KV_LORA + half]
    x2 = w_dkv[:, KV_LORA + half:]
    z = jnp.zeros((w_dkv.shape[0], zpad), w_dkv.dtype)
    w_dkv_p = jnp.concatenate([w_dkv[:, :KV_LORA], x1, x2, z, x2, x1, z], axis=1)
    ukv = w_ukv.reshape(KV_LORA, MLA_HEADS, QK_NOPE + V_HEAD)
    w_uk = ukv[:, :, :QK_NOPE].reshape(KV_LORA, -1)
    w_uv_t = ukv[:, :, QK_NOPE:].reshape(KV_LORA, -1).T
    uq = w_uq.reshape(w_uq.shape[0], MLA_HEADS, QK_NOPE + QK_ROPE)
    q1 = uq[:, :, QK_NOPE:QK_NOPE + half]
    q2 = uq[:, :, QK_NOPE + half:]
    zq = jnp.zeros(q1.shape[:2] + (zpad,), w_uq.dtype)
    flat = lambda t: t.reshape(t.shape[0], -1)
    w_uq_p = jnp.concatenate([flat(uq[:, :, :QK_NOPE]), flat(jnp.concatenate([q1, q2, zq], axis=2)),
                              flat(jnp.concatenate([q2, q1, zq], axis=2))], axis=1)
    return dict(kv_norm_g=kv_norm_g[None, :], w_dkv=w_dkv_p.astype(BF16), kv_latent_g=kv_latent_g[None, :],
                w_uk=w_uk.astype(BF16), w_uv_t=w_uv_t.astype(BF16), w_dq=w_dq.astype(BF16), q_latent_g=q_latent_g[None, :],
                w_uq=w_uq_p.astype(BF16), w_o=w_o.astype(BF16))


def _rope_consts():
    half = QK_ROPE // 2
    inv_freq = ROPE_THETA ** (-jnp.arange(0, QK_ROPE, 2, dtype=F32) / QK_ROPE)
    reps = LANES // half
    freq = jnp.tile(inv_freq, reps)[None, :]
    sign = jnp.tile(jnp.concatenate([-jnp.ones((half,), F32), jnp.ones((half,), F32)]), reps // 2)[None, :]
    return freq, sign


def kernel(x, positions, norm_g, ffn_w_gate, ffn_w_up, ffn_w_down, rwkv_mix, rwkv_w_r, rwkv_w_k, rwkv_w_v, rwkv_w_o, rwkv_w0, rwkv_w1, rwkv_w2, rwkv_a0, rwkv_a1, rwkv_a2, rwkv_g1, rwkv_g2, rwkv_k_k, rwkv_k_a, rwkv_r_k, rwkv_gn_w, rwkv_gn_b, kv_norm_g, mla_w_dkv, mla_kv_latent_g, mla_w_ukv, mla_w_dq, mla_q_latent_g, mla_w_uq, mla_w_o, final_norm_g):
    batch, seq, d = x.shape
    depth = norm_g.shape[0]
    n_a = rwkv_mix.shape[0]
    m = batch * seq
    assert seq % TOKEN_TILE == 0 and seq % ATTN_TILE == 0 and seq % WKV_CHUNK == 0 and d % MXU_DIM == 0

    wg, wu, wd = ffn_w_gate.astype(BF16), ffn_w_up.astype(BF16), ffn_w_down.astype(BF16)
    blkmask = jnp.asarray(np.kron(np.eye(WKV_GROUP), np.ones((RWKV_HEAD, RWKV_HEAD))), BF16)
    freq, sign = _rope_consts()
    cos, sin = _rope_tables(positions.reshape(m, 1), freq, sign)
    mla = None

    h = x.reshape(m, d)
    kv = None
    for layer in range(depth):
        norm = lambda j: norm_g[layer, j][None, :]
        if layer == n_a:
            mla = _prep_mla(kv_norm_g, mla_w_dkv, mla_kv_latent_g, mla_w_ukv, mla_w_dq[0], mla_q_latent_g[0],
                            mla_w_uq[0], mla_w_o[0])
            kv = _mla_kv(h, mla, cos, sin)
        h = _ffn(h, norm(0), wg[layer, 0], wu[layer, 0], wd[layer, 0])
        last = layer == depth - 1
        if layer < n_a:
            i = layer
            p = _prep_rwkv(rwkv_mix[i], rwkv_w_r[i], rwkv_w_k[i], rwkv_w_v[i], rwkv_w_o[i], rwkv_w0[i], rwkv_w1[i],
                           rwkv_w2[i], rwkv_a0[i], rwkv_a1[i], rwkv_a2[i], rwkv_g1[i], rwkv_g2[i], rwkv_k_k[i],
                           rwkv_k_a[i], rwkv_r_k[i], rwkv_gn_w[i], rwkv_gn_b[i])
            r, k, v, kk, b, lw, gate, bonus = _rwkv_proj(h, norm(1), p, seq)
            y = _wkv(lw, r, k, v, kk, b, blkmask, batch, seq)
            h = _rwkv_out(y, bonus, gate, h, p)
            pre = None
        else:
            j = layer - n_a
            if j > 0:
                mla = dict(mla, **{n: v_ for n, v_ in _prep_mla(
                    kv_norm_g, mla_w_dkv, mla_kv_latent_g, mla_w_ukv, mla_w_dq[j], mla_q_latent_g[j], mla_w_uq[j],
                    mla_w_o[j]).items() if n in ("w_dq", "q_latent_g", "w_uq", "w_o")})
            qn, qr = _mla_q(h, norm(1), mla, cos, sin)
            o = _attention(qn, qr, kv[0], kv[1], kv[2], batch, seq)
            pre = (o, mla["w_o"])
        h = _ffn(h, norm(2), wg[layer, 1], wu[layer, 1], wd[layer, 1], pre=pre,
                 final_g=final_norm_g[None, :] if last else None)
    return h.reshape(batch, seq, d)
```

```python
import functools

import numpy as np
import jax
import jax.numpy as jnp
from jax import lax
from jax.experimental import pallas as pl
from jax.experimental.pallas import tpu as pltpu

F32, BF16 = jnp.float32, jnp.bfloat16

RWKV_HEAD = 64
DECAY_LORA = 64
AAA_LORA = 64
GATE_LORA = 128
GN_EPS = 64e-5
RMS_EPS = 1e-6
MLA_HEADS = 8
QK_NOPE = 128
QK_ROPE = 64
V_HEAD = 128
KV_LORA = 256
ROPE_THETA = 10000.0

LANES = 128
MXU_DIM = 256
VMEM_LIMIT = 56 * 1024 * 1024

TOKEN_TILE = 512
PROJ_ROW_GROUPS = 2
FFN_TILE = 1024
FFN_ROW_GROUPS = 2
WKV_CHUNK = 64
WKV_GROUP = MXU_DIM // RWKV_HEAD
WKV_SUBCHUNKS = 4
ATTN_TILE = 1024
ATTN_CHAIN = 256


def _params(*sem):
    return pltpu.CompilerParams(dimension_semantics=sem, vmem_limit_bytes=VMEM_LIMIT)


def _resident(shape):
    return pl.BlockSpec(shape, lambda *_: (0,) * len(shape), pipeline_mode=pl.Buffered(1))


def _rows(tm, d):
    return pl.BlockSpec((tm, d), lambda i: (i, 0))


def _dot(a, b):
    return jnp.dot(a.astype(BF16), b.astype(BF16), preferred_element_type=F32)


def _dot_nt(a, b):
    return lax.dot_general(a.astype(BF16), b.astype(BF16), (((1,), (1,)), ((), ())),
                           preferred_element_type=F32)


def _dot_hilo(a, b):
    hi = a.astype(BF16)
    lo = (a - hi.astype(F32)).astype(BF16)
    return (jnp.dot(hi, b, preferred_element_type=F32) + jnp.dot(lo, b, preferred_element_type=F32))


def _rms(x, g):
    return x * lax.rsqrt(jnp.mean(x * x, axis=-1, keepdims=True) + RMS_EPS) * g


def _ffn_body(*refs, n_chunks, pre_proj, final):
    refs = list(refs)
    x_ref = refs.pop(0)
    if pre_proj:
        o_ref_in, wo_ref = refs.pop(0), refs.pop(0)
    g_ref, wg_ref, wu_ref, wd_ref = refs[:4]
    out_ref = refs[-1]
    tiles = wg_ref.shape[1] // MXU_DIM
    bounds = [MXU_DIM * (tiles * c // n_chunks) for c in range(n_chunks + 1)]
    chunks = [slice(lo, hi) for lo, hi in zip(bounds, bounds[1:])]
    rows = x_ref.shape[0] // FFN_ROW_GROUPS
    groups = [slice(r * rows, (r + 1) * rows) for r in range(FFN_ROW_GROUPS)]
    xs = []
    for rs in groups:
        x = x_ref[rs, :]
        if pre_proj:
            x = x + jnp.dot(o_ref_in[rs, :], wo_ref[...], preferred_element_type=F32)
        xs.append(x)
    xn = [_rms(x, g_ref[...]).astype(BF16) for x in xs]
    gu = [[(jnp.dot(v, wg_ref[:, sl], preferred_element_type=F32),
            jnp.dot(v, wu_ref[:, sl], preferred_element_type=F32)) for sl in chunks] for v in xn]
    for rs, x, gu_r in zip(groups, xs, gu):
        acts = [(gate * jax.nn.sigmoid(gate) * up).astype(BF16) for gate, up in gu_r]
        acc = sum(jnp.dot(a, wd_ref[sl, :], preferred_element_type=F32) for a, sl in zip(acts, chunks))
        y = x + 0.5 * acc
        if final:
            y = _rms(y, refs[4][...])
        out_ref[rs, :] = y


def _ffn(x, g, wg, wu, wd, *, pre=None, final_g=None):
    m, d = x.shape
    f = wg.shape[1]
    tm = FFN_TILE
    ins, specs = [x], [_rows(tm, d)]
    if pre is not None:
        o, wo = pre
        ins += [o, wo]
        specs += [_rows(tm, o.shape[1]), _resident(wo.shape)]
    ins += [g, wg, wu, wd]
    specs += [_resident((1, d)), _resident((d, f)), _resident((d, f)), _resident((f, d))]
    if final_g is not None:
        ins.append(final_g)
        specs.append(_resident((1, d)))
    body = functools.partial(_ffn_body, n_chunks=3, pre_proj=pre is not None, final=final_g is not None)
    return pl.pallas_call(
        body, grid=(m // tm,), in_specs=specs, out_specs=_rows(tm, d),
        out_shape=jax.ShapeDtypeStruct((m, d), F32), compiler_params=_params("parallel"),
        name="ffn")(*ins)


def _softplus(z):
    return jnp.maximum(z, 0.0) + jnp.log(1.0 + jnp.exp(-jnp.abs(z)))


def _bcast_hilo(s, et2):
    hi = s.astype(BF16)
    lo = (s - hi.astype(F32)).astype(BF16)
    return jnp.dot(jnp.concatenate([hi, lo], axis=1), et2, preferred_element_type=F32)


def _rwkv_proj_body(h_ref, hp_ref, g_ref, mix_ref, wr_ref, wk_ref, wv_ref, wl1_ref, wl2_ref, w0_ref, a0_ref,
                    kk_ref, ka_ref, rk_ref, e_ref, et_ref, et2_ref,
                    r_out, k_out, v_out, kkn_out, b_out, lw_out, gate_out, bonus_out, *, tiles_per_seq):
    i = pl.program_id(0)
    gn = g_ref[...]
    hn = _rms(h_ref[...], gn)
    tm, d = hn.shape
    prev = _rms(hp_ref[...], gn)[7:8, :]
    prev = jnp.where(i % tiles_per_seq == 0, 0.0, prev)
    shifted = pltpu.roll(hn, 1, 0)
    row = lax.broadcasted_iota(jnp.int32, (tm, 1), 0)
    shifted = jnp.where(row == 0, prev, shifted)
    xx = shifted - hn
    e, et = e_ref[...], et_ref[...]
    lane = lax.broadcasted_iota(jnp.int32, (1, wl1_ref.shape[1]), 1)

    def project(rs):
        hn_g, xx_g = hn[rs], xx[rs]
        xr, xk, xv = (hn_g + xx_g * mix_ref[c:c + 1, :] for c in (0, 2, 3))
        r = _dot(xr, wr_ref[...])
        k = _dot(xk, wk_ref[...])
        v = _dot(xv, wv_ref[...])
        l1 = jnp.dot(jnp.concatenate([hn_g, xx_g], axis=1).astype(BF16), wl1_ref[...], preferred_element_type=F32)
        act = jnp.where(lane < DECAY_LORA, jnp.tanh(l1),
                        jnp.where(lane < DECAY_LORA + AAA_LORA, l1, jax.nn.sigmoid(l1)))
        return r, k, v, _dot(act, wl2_ref[...])

    def finish(rs, r, k, v, l2):
        w_log = -_softplus(-(w0_ref[...] + l2[:, :d])) - 0.5
        lw_out[rs, :] = -jnp.exp(w_log)
        a = jax.nn.sigmoid(a0_ref[...] + l2[:, d:2 * d])
        gate_out[rs, :] = l2[:, 2 * d:].astype(BF16)
        kk = k * kk_ref[...]
        inv = lax.rsqrt(jnp.maximum(_dot(kk * kk, e), 1e-24))
        kk = kk * _bcast_hilo(inv, et2_ref[...])
        k = k * (1.0 + (a - 1.0) * ka_ref[...])
        r_out[rs, :] = r.astype(BF16)
        k_out[rs, :] = k.astype(BF16)
        v_out[rs, :] = v.astype(BF16)
        kkn_out[rs, :] = kk.astype(BF16)
        b_out[rs, :] = (kk * a).astype(BF16)
        bonus_out[rs, :] = (_dot(_dot(r * k * rk_ref[...], e), et) * v).astype(BF16)

    rows = tm // PROJ_ROW_GROUPS
    groups = [slice(g_ * rows, (g_ + 1) * rows) for g_ in range(PROJ_ROW_GROUPS)]
    projected = [project(rs) for rs in groups]
    for rs, vals in zip(groups, projected):
        finish(rs, *vals)


def _rwkv_proj(h, g, p, seq):
    m, d = h.shape
    tm = TOKEN_TILE
    vec = _resident((1, d))
    in_specs = ([_rows(tm, d), pl.BlockSpec((8, d), lambda i: (jnp.maximum(i * (tm // 8) - 1, 0), 0)),
                 vec, _resident((6, d)), _resident((d, d)), _resident((d, d)), _resident((d, d)),
                 _resident(p["wl1"].shape), _resident(p["wl2"].shape), vec, vec, vec, vec, vec,
                 _resident(p["e"].shape), _resident(p["et"].shape), _resident(p["et2"].shape)])
    f32o = jax.ShapeDtypeStruct((m, d), F32)
    bf16o = jax.ShapeDtypeStruct((m, d), BF16)
    body = functools.partial(_rwkv_proj_body, tiles_per_seq=seq // tm)
    return pl.pallas_call(
        body, grid=(m // tm,), in_specs=in_specs, out_specs=[_rows(tm, d)] * 8,
        out_shape=[bf16o, bf16o, bf16o, bf16o, bf16o, f32o, bf16o, bf16o],
        compiler_params=_params("parallel"), name="rwkv_proj")(
        h, h, g, p["mix"], p["wr"], p["wk"], p["wv"], p["wl1"], p["wl2"], p["w0"], p["a0"],
        p["k_k"], p["k_a"], p["r_k"], p["e"], p["et"], p["et2"])


def _blk(x, blkmask):
    return jnp.concatenate([x] * WKV_GROUP, axis=0) * blkmask


def _wkv_body(lw_ref, r_ref, k_ref, v_ref, kk_ref, b_ref, blk_ref, y_ref, h_ref, *, n_sub):
    c = WKV_CHUNK
    gw = MXU_DIM

    @pl.when(pl.program_id(1) == 0)
    def _():
        h_ref[...] = jnp.zeros_like(h_ref)

    blkmask = blk_ref[...]
    row = lax.broadcasted_iota(jnp.int32, (c, gw), 0)
    colj = lax.broadcasted_iota(jnp.int32, (c, gw), 1) % c
    strict, incl = row > colj, row >= colj
    eye_w = (row == colj).astype(F32)
    eye_b = eye_w.astype(BF16)
    r2 = lax.broadcasted_iota(jnp.int32, (c, c), 0)
    c2 = lax.broadcasted_iota(jnp.int32, (c, c), 1)
    tri = (r2 >= c2).astype(BF16)

    blk = lambda x: _blk(x, blkmask)
    mm = lambda a, b: jnp.dot(a, b, preferred_element_type=F32)
    stack = lambda *a: jnp.concatenate(a, axis=0)
    side = lambda *a: jnp.concatenate(a, axis=1)
    groups = [slice(g * gw, (g + 1) * gw) for g in range(h_ref.shape[1] // gw)]

    a_t, r_f, r_t, b_t, k_t, b_c, k_c, vg, w_end = ([] for _ in range(9))
    for ci in range(n_sub):
        rows = slice(ci * c, (ci + 1) * c)
        lw = lw_ref[rows, :]
        cs = _dot_hilo_left(tri, lw)
        cs_end = cs[c - 1:c, :]
        w_inv = jnp.exp(-cs)
        w_rem = jnp.exp(cs_end - cs)
        a_all = (-kk_ref[rows, :].astype(F32) * jnp.exp(cs - lw)).astype(BF16)
        r_all = r_ref[rows, :].astype(F32) * jnp.exp(cs)
        b_raw, k_raw, v_all = b_ref[rows, :].astype(F32), k_ref[rows, :].astype(F32), v_ref[rows, :]
        bt_all, kt_all = (b_raw * w_inv).astype(BF16), (k_raw * w_inv).astype(BF16)
        bc_all, kc_all = (b_raw * w_rem).astype(BF16), (k_raw * w_rem).astype(BF16)
        we_all = jnp.exp(cs_end)
        for sl in groups:
            a_t.append(a_all[:, sl])
            r_f.append(r_all[:, sl])
            r_t.append(r_all[:, sl].astype(BF16))
            b_t.append(bt_all[:, sl])
            k_t.append(kt_all[:, sl])
            b_c.append(bc_all[:, sl])
            k_c.append(kc_all[:, sl])
            vg.append(v_all[:, sl])
            w_end.append(we_all[:, sl])

    vblk = [blk(v) for v in vg]
    ar = [stack(a, r) for a, r in zip(a_t, r_t)]
    sb = [_dot_nt(x, blk(b)) for x, b in zip(ar, b_t)]
    sk = [_dot_nt(x, blk(k)) for x, k in zip(ar, k_t)]
    bkt = [_dot_nt(eye_b, stack(blk(b), blk(k))).astype(BF16) for b, k in zip(b_c, k_c)]
    lab = [jnp.where(strict, s[:c], 0.0) for s in sb]
    mrb = [jnp.where(incl, s[c:], 0.0).astype(BF16) for s in sb]
    lmk = [stack(jnp.where(strict, s[:c], 0.0).astype(BF16), jnp.where(incl, s[c:], 0.0).astype(BF16), u[:, gw:])
           for s, u in zip(sk, bkt)]
    t = [eye_w + x for x in lab]
    x = [v.astype(BF16) for v in lab]
    x = [mm(v, blk(v)).astype(BF16) for v in x]
    for _ in range(4):
        tx = [mm(stack(tv.astype(BF16), xv), blk(xv)) for tv, xv in zip(t, x)]
        t = [tv + u[:c] for tv, u in zip(t, tx)]
        x = [u[c:].astype(BF16) for u in tx]
    t = [tv + mm(tv.astype(BF16), blk(xv)) for tv, xv in zip(t, x)]
    lmkv = [mm(l_, v) for l_, v in zip(lmk, vblk)]
    mt = [mm(stack(m_, u[:, :gw]), blk(tv.astype(BF16))).astype(BF16) for m_, u, tv in zip(mrb, bkt, t)]
    ryg = [mm(w, side(blk(a), blk(u[:c].astype(BF16)))) for w, a, u in zip(mt, a_t, lmkv)]
    r_hat = [rf + u[:c, :gw] for rf, u in zip(r_f, ryg)]
    y_hat = [u[:c, gw:] + w[c:2 * c] for u, w in zip(ryg, lmkv)]
    g_w = [eye_w * we + u[c:, :gw] for we, u in zip(w_end, ryg)]
    z_w = [u[c:, gw:] + w[2 * c:] for u, w in zip(ryg, lmkv)]
    lhs = [stack(rh, gv).astype(BF16) for rh, gv in zip(r_hat, g_w)]
    state = [h_ref[:, sl] for sl in groups]
    for ci in range(n_sub):
        for gi, sl in enumerate(groups):
            i = ci * len(groups) + gi
            u = mm(lhs[i], blk(state[gi].astype(BF16)))
            y_ref[ci * c:(ci + 1) * c, sl] = u[:c] + y_hat[i]
            state[gi] = u[c:] + z_w[i]
    for gi, sl in enumerate(groups):
        h_ref[:, sl] = state[gi]


def _dot_hilo_left(a, b):
    hi = b.astype(BF16)
    lo = (b - hi.astype(F32)).astype(BF16)
    return jnp.dot(a, hi, preferred_element_type=F32) + jnp.dot(a, lo, preferred_element_type=F32)


def _wkv(lw, r, k, v, kk, b, blkmask, batch, seq):
    m, d = r.shape
    rows = WKV_CHUNK * WKV_SUBCHUNKS
    nc = seq // rows
    spec = pl.BlockSpec((rows, d), lambda bi, ci: (bi * nc + ci, 0))
    return pl.pallas_call(
        functools.partial(_wkv_body, n_sub=WKV_SUBCHUNKS), grid=(batch, nc),
        in_specs=[spec] * 6 + [_resident(blkmask.shape)], out_specs=spec,
        out_shape=jax.ShapeDtypeStruct((m, d), F32),
        scratch_shapes=[pltpu.VMEM((RWKV_HEAD, d), F32)],
        compiler_params=_params("parallel", "arbitrary"), name="wkv")(lw, r, k, v, kk, b, blkmask)


def _rwkv_out_body(y_ref, bonus_ref, gate_ref, h_ref, gnw_ref, gnb_ref, wo_ref, e_ref, et2_ref, out_ref):
    e, et2 = e_ref[...], et2_ref[...]
    y = y_ref[...]
    inv_n = 1.0 / RWKV_HEAD
    mu = _bcast_hilo(_dot(y, e) * inv_n, et2)
    dlt = y - mu
    var = _dot(dlt * dlt, e) * inv_n
    rstd = _bcast_hilo(lax.rsqrt(var + GN_EPS), et2)
    yn = dlt * rstd * gnw_ref[...] + gnb_ref[...]
    out = (yn + bonus_ref[...].astype(F32)) * gate_ref[...].astype(F32)
    out_ref[...] = h_ref[...] + _dot(out, wo_ref[...])


def _rwkv_out(y, bonus, gate, h, p):
    m, d = h.shape
    tm = TOKEN_TILE
    vec = _resident((1, d))
    return pl.pallas_call(
        _rwkv_out_body, grid=(m // tm,),
        in_specs=[_rows(tm, d)] * 4 + [vec, vec, _resident((d, d)), _resident(p["e"].shape), _resident(p["et2"].shape)],
        out_specs=_rows(tm, d), out_shape=jax.ShapeDtypeStruct((m, d), F32),
        compiler_params=_params("parallel"), name="rwkv_out")(
        y, bonus, gate, h, p["gn_w"], p["gn_b"], p["wo"], p["e"], p["et2"])


def _rope_body(pos_ref, freq_ref, sign_ref, cos_ref, sin_ref):
    ang = pos_ref[...].astype(F32) * freq_ref[...]
    cos_ref[...] = jnp.cos(ang)
    sin_ref[...] = jnp.sin(ang) * sign_ref[...]


def _rope_tables(pos, freq, sign):
    m = pos.shape[0]
    tm = TOKEN_TILE
    out = jax.ShapeDtypeStruct((m, LANES), F32)
    return pl.pallas_call(
        _rope_body, grid=(m // tm,),
        in_specs=[_rows(tm, 1), _resident((1, LANES)), _resident((1, LANES))],
        out_specs=[_rows(tm, LANES)] * 2, out_shape=[out, out],
        compiler_params=_params("parallel"), name="rope_tables")(pos, freq, sign)


def _mla_kv_body(h_ref, g_ref, wd_ref, lg_ref, wuk_ref, wuvt_ref, cos_ref, sin_ref, kn_out, kr_out, vt_out):
    hn = _rms(h_ref[...], g_ref[...])
    ckv = _dot(hn, wd_ref[...])
    lat = _rms(ckv[:, :KV_LORA], lg_ref[...]).astype(BF16)
    kn_out[...] = jnp.dot(lat, wuk_ref[...], preferred_element_type=F32).astype(BF16)
    vt_out[...] = _dot_nt(wuvt_ref[...], lat).astype(BF16)
    kr = ckv[:, KV_LORA:KV_LORA + LANES]
    krs = ckv[:, KV_LORA + LANES:]
    kr_out[...] = (kr * cos_ref[...] + krs * sin_ref[...]).astype(BF16)


def _mla_kv(h, p, cos, sin):
    m, d = h.shape
    tm = TOKEN_TILE
    hv = MLA_HEADS * V_HEAD
    return pl.pallas_call(
        _mla_kv_body, grid=(m // tm,),
        in_specs=[_rows(tm, d), _resident((1, d)), _resident(p["w_dkv"].shape), _resident((1, KV_LORA)),
                  _resident(p["w_uk"].shape), _resident(p["w_uv_t"].shape), _rows(tm, LANES), _rows(tm, LANES)],
        out_specs=[_rows(tm, hv), _rows(tm, LANES), pl.BlockSpec((hv, tm), lambda i: (0, i))],
        out_shape=[jax.ShapeDtypeStruct((m, hv), BF16), jax.ShapeDtypeStruct((m, LANES), BF16),
                   jax.ShapeDtypeStruct((hv, m), BF16)],
        compiler_params=_params("parallel"), name="mla_kv")(
        h, p["kv_norm_g"], p["w_dkv"], p["kv_latent_g"], p["w_uk"], p["w_uv_t"], cos, sin)


def _mla_q_body(h_ref, g_ref, wd_ref, lg_ref, wu_ref, cos_ref, sin_ref, qn_out, qr_out, *, scale):
    hn = _rms(h_ref[...], g_ref[...])
    lat = _rms(_dot(hn, wd_ref[...]), lg_ref[...])
    q = _dot(lat, wu_ref[...]) * scale
    w = q.shape[1] // 3
    cos = jnp.concatenate([cos_ref[...]] * MLA_HEADS, axis=1)
    sin = jnp.concatenate([sin_ref[...]] * MLA_HEADS, axis=1)
    qn_out[...] = q[:, :w].astype(BF16)
    qr_out[...] = (q[:, w:2 * w] * cos + q[:, 2 * w:] * sin).astype(BF16)


def _mla_q(h, g, p, cos, sin):
    m, d = h.shape
    tm = TOKEN_TILE
    w = MLA_HEADS * LANES
    scale = float(QK_NOPE + QK_ROPE) ** -0.5 * float(np.log2(np.e))
    out = jax.ShapeDtypeStruct((m, w), BF16)
    return pl.pallas_call(
        functools.partial(_mla_q_body, scale=scale), grid=(m // tm,),
        in_specs=[_rows(tm, d), _resident((1, d)), _resident(p["w_dq"].shape), _resident(p["q_latent_g"].shape),
                  _resident(p["w_uq"].shape), _rows(tm, LANES), _rows(tm, LANES)],
        out_specs=[_rows(tm, w)] * 2, out_shape=[out, out],
        compiler_params=_params("parallel"), name="mla_q")(
        h, g, p["w_dq"], p["q_latent_g"], p["w_uq"], cos, sin)


def _attn_body(qn_ref, qr_ref, kn_ref, kr_ref, vt_ref, o_ref):
    t = ATTN_TILE
    rows = ATTN_CHAIN
    n_chain = t // rows
    n_tiles = qn_ref.shape[0] // t

    def queries(qt, i):
        lo = qt * t + i * rows
        return jnp.concatenate([qn_ref[lo:lo + rows, :], qr_ref[lo:lo + rows, :]], axis=1)

    def keys(kt, n):
        return jnp.concatenate([kn_ref[kt * t:kt * t + n, :], kr_ref[kt * t:kt * t + n, :]], axis=1)

    def n_keys(qt, kt, i):
        return t if kt < qt else (i + 1) * rows

    def scores(qt, kt):
        out = []
        for i in range(n_chain):
            n = n_keys(qt, kt, i)
            s_t = _dot_nt(keys(kt, n), queries(qt, i))
            if kt == qt:
                mask = (lax.broadcasted_iota(jnp.int32, (n, rows), 1) + i * rows
                        >= lax.broadcasted_iota(jnp.int32, (n, rows), 0))
                s_t = jnp.where(mask, s_t, -1e30)
            out.append(s_t)
        return out

    def update(carry, s_t, v_t):
        m_i, l_i, acc = carry
        m_new = jnp.maximum(m_i, jnp.max(s_t, axis=0, keepdims=True))
        p_t = jnp.exp2(s_t - m_new)
        alpha = jnp.exp2(m_i - m_new)
        l_new = alpha * l_i + jnp.sum(p_t, axis=0, keepdims=True)
        return m_new, l_new, alpha * acc + jnp.dot(v_t, p_t.astype(BF16), preferred_element_type=F32)

    steps = [(qt, kt) for qt in range(n_tiles) for kt in range(qt + 1)]
    one = (jnp.full((1, rows), -jnp.inf, F32), jnp.zeros((1, rows), F32), jnp.zeros((V_HEAD, rows), F32))
    state = [[one] * n_chain for _ in range(n_tiles)]
    s_next = scores(*steps[0])
    for n, (qt, kt) in enumerate(steps):
        s_cur = s_next
        if n + 1 < len(steps):
            s_next = scores(*steps[n + 1])
        for i in range(n_chain):
            nk = n_keys(qt, kt, i)
            state[qt][i] = update(state[qt][i], s_cur[i], vt_ref[:, kt * t:kt * t + nk])
            if kt == qt:
                _, l_i, acc = state[qt][i]
                lo = qt * t + i * rows
                o_ref[lo:lo + rows, :] = (acc / l_i).T.astype(BF16)


def _attention(qn, qr, kn, kr, vt, batch, seq):
    m = qn.shape[0]
    spec = pl.BlockSpec((seq, LANES), lambda b, h: (b, h))
    krspec = pl.BlockSpec((seq, LANES), lambda b, h: (b, 0))
    vtspec = pl.BlockSpec((V_HEAD, seq), lambda b, h: (h, b))
    return pl.pallas_call(
        _attn_body, grid=(batch, MLA_HEADS),
        in_specs=[spec, spec, spec, krspec, vtspec], out_specs=spec,
        out_shape=jax.ShapeDtypeStruct((m, MLA_HEADS * V_HEAD), BF16),
        compiler_params=_params("parallel", "parallel"), name="mla_attention")(qn, qr, kn, kr, vt)


def _prep_rwkv(mix, w_r, w_k, w_v, w_o, w0, w1, w2, a0, a1, a2, g1, g2, k_k, k_a, r_k, gn_w, gn_b):
    d = w_r.shape[0]
    assert (w1.shape[1], a1.shape[1], g1.shape[1]) == (DECAY_LORA, AAA_LORA, GATE_LORA)
    head_of = np.arange(d) // RWKV_HEAD
    e = (head_of[:, None] == np.arange(LANES)[None, :]).astype(np.float32)
    first = jnp.concatenate([w1, a1, g1], axis=1)
    shift = jnp.concatenate([mix[1][:, None] * w1, mix[4][:, None] * a1, mix[5][:, None] * g1], axis=1)
    wl1 = jnp.concatenate([first, shift], axis=0)
    zeros = lambda n: jnp.zeros((n, d), w2.dtype)
    wl2 = jnp.concatenate([
        jnp.concatenate([w2, zeros(AAA_LORA), zeros(GATE_LORA)], axis=0),
        jnp.concatenate([zeros(DECAY_LORA), a2, zeros(GATE_LORA)], axis=0),
        jnp.concatenate([zeros(DECAY_LORA), zeros(AAA_LORA), g2], axis=0)], axis=1)
    return dict(
        mix=mix, wr=w_r.astype(BF16), wk=w_k.astype(BF16), wv=w_v.astype(BF16), wo=w_o.astype(BF16),
        wl1=wl1.astype(BF16), wl2=wl2.astype(BF16), w0=w0[None, :], a0=a0[None, :],
        k_k=k_k[None, :], k_a=k_a[None, :], r_k=r_k.reshape(1, d),
        gn_w=gn_w[None, :], gn_b=gn_b[None, :], e=jnp.asarray(e, BF16), et=jnp.asarray(e.T, BF16),
        et2=jnp.asarray(np.concatenate([e.T, e.T], axis=0), BF16))


def _prep_mla(kv_norm_g, w_dkv, kv_latent_g, w_ukv, w_dq, q_latent_g, w_uq, w_o):
    half = QK_ROPE // 2
    zpad = LANES - QK_ROPE
    x1 = w_dkv[:, KV_LORA:KV_LORA + half]
    x2 = w_dkv[:, KV_LORA + half:]
    z = jnp.zeros((w_dkv.shape[0], zpad), w_dkv.dtype)
    w_dkv_p = jnp.concatenate([w_dkv[:, :KV_LORA], x1, x2, z, x2, x1, z], axis=1)
    ukv = w_ukv.reshape(KV_LORA, MLA_HEADS, QK_NOPE + V_HEAD)
    w_uk = ukv[:, :, :QK_NOPE].reshape(KV_LORA, -1)
    w_uv_t = ukv[:, :, QK_NOPE:].reshape(KV_LORA, -1).T
    uq = w_uq.reshape(w_uq.shape[0], MLA_HEADS, QK_NOPE + QK_ROPE)
    q1 = uq[:, :, QK_NOPE:QK_NOPE + half]
    q2 = uq[:, :, QK_NOPE + half:]
    zq = jnp.zeros(q1.shape[:2] + (zpad,), w_uq.dtype)
    flat = lambda t: t.reshape(t.shape[0], -1)
    w_uq_p = jnp.concatenate([flat(uq[:, :, :QK_NOPE]), flat(jnp.concatenate([q1, q2, zq], axis=2)),
                              flat(jnp.concatenate([q2, q1, zq], axis=2))], axis=1)
    return dict(kv_norm_g=kv_norm_g[None, :], w_dkv=w_dkv_p.astype(BF16), kv_latent_g=kv_latent_g[None, :],
                w_uk=w_uk.astype(BF16), w_uv_t=w_uv_t.astype(BF16), w_dq=w_dq.astype(BF16), q_latent_g=q_latent_g[None, :],
                w_uq=w_uq_p.astype(BF16), w_o=w_o.astype(BF16))


def _rope_consts():
    half = QK_ROPE // 2
    inv_freq = ROPE_THETA ** (-jnp.arange(0, QK_ROPE, 2, dtype=F32) / QK_ROPE)
    reps = LANES // half
    freq = jnp.tile(inv_freq, reps)[None, :]
    sign = jnp.tile(jnp.concatenate([-jnp.ones((half,), F32), jnp.ones((half,), F32)]), reps // 2)[None, :]
    return freq, sign


def kernel(x, positions, norm_g, ffn_w_gate, ffn_w_up, ffn_w_down, rwkv_mix, rwkv_w_r, rwkv_w_k, rwkv_w_v, rwkv_w_o, rwkv_w0, rwkv_w1, rwkv_w2, rwkv_a0, rwkv_a1, rwkv_a2, rwkv_g1, rwkv_g2, rwkv_k_k, rwkv_k_a, rwkv_r_k, rwkv_gn_w, rwkv_gn_b, kv_norm_g, mla_w_dkv, mla_kv_latent_g, mla_w_ukv, mla_w_dq, mla_q_latent_g, mla_w_uq, mla_w_o, final_norm_g):
    batch, seq, d = x.shape
    depth = norm_g.shape[0]
    n_a = rwkv_mix.shape[0]
    m = batch * seq
    assert seq % TOKEN_TILE == 0 and seq % ATTN_TILE == 0 and seq % WKV_CHUNK == 0 and d % MXU_DIM == 0

    wg, wu, wd = ffn_w_gate.astype(BF16), ffn_w_up.astype(BF16), ffn_w_down.astype(BF16)
    blkmask = jnp.asarray(np.kron(np.eye(WKV_GROUP), np.ones((RWKV_HEAD, RWKV_HEAD))), BF16)
    freq, sign = _rope_consts()
    cos, sin = _rope_tables(positions.reshape(m, 1), freq, sign)
    mla = None

    h = x.reshape(m, d)
    kv = None
    for layer in range(depth):
        norm = lambda j: norm_g[layer, j][None, :]
        if layer == n_a:
            mla = _prep_mla(kv_norm_g, mla_w_dkv, mla_kv_latent_g, mla_w_ukv, mla_w_dq[0], mla_q_latent_g[0],
                            mla_w_uq[0], mla_w_o[0])
            kv = _mla_kv(h, mla, cos, sin)
        h = _ffn(h, norm(0), wg[layer, 0], wu[layer, 0], wd[layer, 0])
        last = layer == depth - 1
        if layer < n_a:
            i = layer
            p = _prep_rwkv(rwkv_mix[i], rwkv_w_r[i], rwkv_w_k[i], rwkv_w_v[i], rwkv_w_o[i], rwkv_w0[i], rwkv_w1[i],
                           rwkv_w2[i], rwkv_a0[i], rwkv_a1[i], rwkv_a2[i], rwkv_g1[i], rwkv_g2[i], rwkv_k_k[i],
                           rwkv_k_a[i], rwkv_r_k[i], rwkv_gn_w[i], rwkv_gn_b[i])
            r, k, v, kk, b, lw, gate, bonus = _rwkv_proj(h, norm(1), p, seq)
            y = _wkv(lw, r, k, v, kk, b, blkmask, batch, seq)
            h = _rwkv_out(y, bonus, gate, h, p)
            pre = None
        else:
            j = layer - n_a
            if j > 0:
                mla = dict(mla, **{n: v_ for n, v_ in _prep_mla(
                    kv_norm_g, mla_w_dkv, mla_kv_latent_g, mla_w_ukv, mla_w_dq[j], mla_q_latent_g[j], mla_w_uq[j],
                    mla_w_o[j]).items() if n in ("w_dq", "q_latent_g", "w_uq", "w_o")})
            qn, qr = _mla_q(h, norm(1), mla, cos, sin)
            o = _attention(qn, qr, kv[0], kv[1], kv[2], batch, seq)
            pre = (o, mla["w_o"])
        h = _ffn(h, norm(2), wg[layer, 1], wu[layer, 1], wd[layer, 1], pre=pre,
                 final_g=final_norm_g[None, :] if last else None)
    return h.reshape(batch, seq, d)
```

```python
import functools

import numpy as np
import jax
import jax.numpy as jnp
from jax import lax
from jax.experimental import pallas as pl
from jax.experimental.pallas import tpu as pltpu

F32, BF16 = jnp.float32, jnp.bfloat16

RWKV_HEAD = 64
DECAY_LORA = 64
AAA_LORA = 64
GATE_LORA = 128
GN_EPS = 64e-5
RMS_EPS = 1e-6
MLA_HEADS = 8
QK_NOPE = 128
QK_ROPE = 64
V_HEAD = 128
KV_LORA = 256
ROPE_THETA = 10000.0

LANES = 128
MXU_DIM = 256
VMEM_LIMIT = 56 * 1024 * 1024

TOKEN_TILE = 512
LIGHT_TILE = 1024
PROJ_ROW_GROUPS = 2
FFN_TILE = 1024
FFN_ROW_GROUPS = 2
WKV_CHUNK = 64
WKV_GROUP = MXU_DIM // RWKV_HEAD
WKV_SUBCHUNKS = 4
ATTN_TILE = 1024
ATTN_CHAIN = 256
ATTN_LOOKAHEAD = 6


def _params(*sem):
    return pltpu.CompilerParams(dimension_semantics=sem, vmem_limit_bytes=VMEM_LIMIT)


def _resident(shape):
    return pl.BlockSpec(shape, lambda *_: (0,) * len(shape), pipeline_mode=pl.Buffered(1))


def _rows(tm, d):
    return pl.BlockSpec((tm, d), lambda i: (i, 0))


def _dot(a, b):
    return jnp.dot(a.astype(BF16), b.astype(BF16), preferred_element_type=F32)


def _dot_nt(a, b):
    return lax.dot_general(a.astype(BF16), b.astype(BF16), (((1,), (1,)), ((), ())),
                           preferred_element_type=F32)


def _dot_hilo(a, b):
    hi = a.astype(BF16)
    lo = (a - hi.astype(F32)).astype(BF16)
    return (jnp.dot(hi, b, preferred_element_type=F32) + jnp.dot(lo, b, preferred_element_type=F32))


def _rms(x, g):
    return x * lax.rsqrt(jnp.mean(x * x, axis=-1, keepdims=True) + RMS_EPS) * g


def _ffn_body(*refs, n_chunks, pre_proj, final):
    refs = list(refs)
    x_ref = refs.pop(0)
    if pre_proj:
        o_ref_in, wo_ref = refs.pop(0), refs.pop(0)
    g_ref, wg_ref, wu_ref, wd_ref = refs[:4]
    out_ref = refs[-1]
    tiles = wg_ref.shape[1] // MXU_DIM
    bounds = [MXU_DIM * (tiles * c // n_chunks) for c in range(n_chunks + 1)]
    chunks = [slice(lo, hi) for lo, hi in zip(bounds, bounds[1:])]
    rows = x_ref.shape[0] // FFN_ROW_GROUPS
    groups = [slice(r * rows, (r + 1) * rows) for r in range(FFN_ROW_GROUPS)]
    xs = []
    for rs in groups:
        x = x_ref[rs, :]
        if pre_proj:
            x = x + jnp.dot(o_ref_in[rs, :], wo_ref[...], preferred_element_type=F32)
        xs.append(x)
    xn = [_rms(x, g_ref[...]).astype(BF16) for x in xs]
    gu = [[(jnp.dot(v, wg_ref[:, sl], preferred_element_type=F32),
            jnp.dot(v, wu_ref[:, sl], preferred_element_type=F32)) for sl in chunks] for v in xn]
    for rs, x, gu_r in zip(groups, xs, gu):
        acts = [(gate * jax.nn.sigmoid(gate) * up).astype(BF16) for gate, up in gu_r]
        acc = sum(jnp.dot(a, wd_ref[sl, :], preferred_element_type=F32) for a, sl in zip(acts, chunks))
        y = x + 0.5 * acc
        if final:
            y = _rms(y, refs[4][...])
        out_ref[rs, :] = y


def _ffn(x, g, wg, wu, wd, which, *, pre=None, final_g=None):
    m, d = x.shape
    f = wg.shape[-1]
    tm = FFN_TILE
    ins, specs = [x], [_rows(tm, d)]
    if pre is not None:
        o, wo = pre
        ins += [o, wo]
        specs += [_rows(tm, o.shape[1]), _resident(wo.shape)]
    picked = lambda r, c: pl.BlockSpec((None, None, r, c), lambda i: (*which, 0, 0), pipeline_mode=pl.Buffered(1))
    ins += [g, wg, wu, wd]
    specs += [_resident((1, d)), picked(d, f), picked(d, f), picked(f, d)]
    if final_g is not None:
        ins.append(final_g)
        specs.append(_resident((1, d)))
    body = functools.partial(_ffn_body, n_chunks=3, pre_proj=pre is not None, final=final_g is not None)
    return pl.pallas_call(
        body, grid=(m // tm,), in_specs=specs, out_specs=_rows(tm, d),
        out_shape=jax.ShapeDtypeStruct((m, d), F32), compiler_params=_params("parallel"),
        name="ffn")(*ins)


def _softplus(z):
    return jnp.maximum(z, 0.0) + jnp.log(1.0 + jnp.exp(-jnp.abs(z)))


def _bcast_hilo(s, et2):
    hi = s.astype(BF16)
    lo = (s - hi.astype(F32)).astype(BF16)
    return jnp.dot(jnp.concatenate([hi, lo], axis=1), et2, preferred_element_type=F32)


def _rwkv_proj_body(h_ref, hp_ref, g_ref, mix_ref, wr_ref, wk_ref, wv_ref, wl1_ref, wl2_ref, w0_ref, a0_ref,
                    kk_ref, ka_ref, rk_ref, e_ref, et_ref, et2_ref,
                    r_out, k_out, v_out, kkn_out, b_out, lw_out, gate_out, bonus_out, *, tiles_per_seq):
    i = pl.program_id(0)
    gn = g_ref[...]
    hn = _rms(h_ref[...], gn)
    tm, d = hn.shape
    prev = _rms(hp_ref[...], gn)[7:8, :]
    prev = jnp.where(i % tiles_per_seq == 0, 0.0, prev)
    shifted = pltpu.roll(hn, 1, 0)
    row = lax.broadcasted_iota(jnp.int32, (tm, 1), 0)
    shifted = jnp.where(row == 0, prev, shifted)
    xx = shifted - hn
    e, et = e_ref[...], et_ref[...]
    lane = lax.broadcasted_iota(jnp.int32, (1, wl1_ref.shape[1]), 1)

    def project(rs):
        hn_g, xx_g = hn[rs], xx[rs]
        xr, xk, xv = (hn_g + xx_g * mix_ref[c:c + 1, :] for c in (0, 2, 3))
        r = _dot(xr, wr_ref[...])
        k = _dot(xk, wk_ref[...])
        v = _dot(xv, wv_ref[...])
        l1 = jnp.dot(jnp.concatenate([hn_g, xx_g], axis=1).astype(BF16), wl1_ref[...], preferred_element_type=F32)
        act = jnp.where(lane < DECAY_LORA, jnp.tanh(l1),
                        jnp.where(lane < DECAY_LORA + AAA_LORA, l1, jax.nn.sigmoid(l1)))
        return r, k, v, _dot(act, wl2_ref[...])

    def finish(rs, r, k, v, l2):
        w_log = -_softplus(-(w0_ref[...] + l2[:, :d])) - 0.5
        lw_out[rs, :] = -jnp.exp(w_log)
        a = jax.nn.sigmoid(a0_ref[...] + l2[:, d:2 * d])
        gate_out[rs, :] = l2[:, 2 * d:].astype(BF16)
        kk = k * kk_ref[...]
        inv = lax.rsqrt(jnp.maximum(_dot(kk * kk, e), 1e-24))
        kk = kk * _bcast_hilo(inv, et2_ref[...])
        k = k * (1.0 + (a - 1.0) * ka_ref[...])
        r_out[rs, :] = r.astype(BF16)
        k_out[rs, :] = k.astype(BF16)
        v_out[rs, :] = v.astype(BF16)
        kkn_out[rs, :] = kk.astype(BF16)
        b_out[rs, :] = (kk * a).astype(BF16)
        bonus_out[rs, :] = (_dot(_dot(r * k * rk_ref[...], e), et) * v).astype(BF16)

    rows = tm // PROJ_ROW_GROUPS
    groups = [slice(g_ * rows, (g_ + 1) * rows) for g_ in range(PROJ_ROW_GROUPS)]
    projected = [project(rs) for rs in groups]
    for rs, vals in zip(groups, projected):
        finish(rs, *vals)


def _rwkv_proj(h, g, p, seq):
    m, d = h.shape
    tm = TOKEN_TILE
    vec = _resident((1, d))
    in_specs = ([_rows(tm, d), pl.BlockSpec((8, d), lambda i: (jnp.maximum(i * (tm // 8) - 1, 0), 0)),
                 vec, _resident((6, d)), _resident((d, d)), _resident((d, d)), _resident((d, d)),
                 _resident(p["wl1"].shape), _resident(p["wl2"].shape), vec, vec, vec, vec, vec,
                 _resident(p["e"].shape), _resident(p["et"].shape), _resident(p["et2"].shape)])
    f32o = jax.ShapeDtypeStruct((m, d), F32)
    bf16o = jax.ShapeDtypeStruct((m, d), BF16)
    body = functools.partial(_rwkv_proj_body, tiles_per_seq=seq // tm)
    return pl.pallas_call(
        body, grid=(m // tm,), in_specs=in_specs, out_specs=[_rows(tm, d)] * 8,
        out_shape=[bf16o, bf16o, bf16o, bf16o, bf16o, f32o, bf16o, bf16o],
        compiler_params=_params("parallel"), name="rwkv_proj")(
        h, h, g, p["mix"], p["wr"], p["wk"], p["wv"], p["wl1"], p["wl2"], p["w0"], p["a0"],
        p["k_k"], p["k_a"], p["r_k"], p["e"], p["et"], p["et2"])


def _blk(x, blkmask):
    return jnp.concatenate([x] * WKV_GROUP, axis=0) * blkmask


def _wkv_body(lw_ref, r_ref, k_ref, v_ref, kk_ref, b_ref, blk_ref, y_ref, h_ref, *, n_sub):
    c = WKV_CHUNK
    gw = MXU_DIM

    @pl.when(pl.program_id(1) == 0)
    def _():
        h_ref[...] = jnp.zeros_like(h_ref)

    blkmask = blk_ref[...]
    row = lax.broadcasted_iota(jnp.int32, (c, gw), 0)
    colj = lax.broadcasted_iota(jnp.int32, (c, gw), 1) % c
    strict, incl = row > colj, row >= colj
    eye_w = (row == colj).astype(F32)
    eye_b = eye_w.astype(BF16)
    r2 = lax.broadcasted_iota(jnp.int32, (c, c), 0)
    c2 = lax.broadcasted_iota(jnp.int32, (c, c), 1)
    tri = (r2 >= c2).astype(BF16)

    blk = lambda x: _blk(x, blkmask)
    mm = lambda a, b: jnp.dot(a, b, preferred_element_type=F32)
    stack = lambda *a: jnp.concatenate(a, axis=0)
    side = lambda *a: jnp.concatenate(a, axis=1)
    groups = [slice(g * gw, (g + 1) * gw) for g in range(h_ref.shape[1] // gw)]

    a_t, r_f, r_t, b_t, k_t, b_c, k_c, vg, w_end = ([] for _ in range(9))
    for ci in range(n_sub):
        rows = slice(ci * c, (ci + 1) * c)
        lw = lw_ref[rows, :]
        cs = _dot_hilo_left(tri, lw)
        cs_end = cs[c - 1:c, :]
        w_inv = jnp.exp(-cs)
        w_rem = jnp.exp(cs_end - cs)
        a_all = (-kk_ref[rows, :].astype(F32) * jnp.exp(cs - lw)).astype(BF16)
        r_all = r_ref[rows, :].astype(F32) * jnp.exp(cs)
        b_raw, k_raw, v_all = b_ref[rows, :].astype(F32), k_ref[rows, :].astype(F32), v_ref[rows, :]
        bt_all, kt_all = (b_raw * w_inv).astype(BF16), (k_raw * w_inv).astype(BF16)
        bc_all, kc_all = (b_raw * w_rem).astype(BF16), (k_raw * w_rem).astype(BF16)
        we_all = jnp.exp(cs_end)
        for sl in groups:
            a_t.append(a_all[:, sl])
            r_f.append(r_all[:, sl])
            r_t.append(r_all[:, sl].astype(BF16))
            b_t.append(bt_all[:, sl])
            k_t.append(kt_all[:, sl])
            b_c.append(bc_all[:, sl])
            k_c.append(kc_all[:, sl])
            vg.append(v_all[:, sl])
            w_end.append(we_all[:, sl])

    vblk = [blk(v) for v in vg]
    ar = [stack(a, r) for a, r in zip(a_t, r_t)]
    sb = [_dot_nt(x, blk(b)) for x, b in zip(ar, b_t)]
    sk = [_dot_nt(x, blk(k)) for x, k in zip(ar, k_t)]
    bkt = [_dot_nt(eye_b, stack(blk(b), blk(k))).astype(BF16) for b, k in zip(b_c, k_c)]
    lab = [jnp.where(strict, s[:c], 0.0) for s in sb]
    mrb = [jnp.where(incl, s[c:], 0.0).astype(BF16) for s in sb]
    lmk = [stack(jnp.where(strict, s[:c], 0.0).astype(BF16), jnp.where(incl, s[c:], 0.0).astype(BF16), u[:, gw:])
           for s, u in zip(sk, bkt)]
    t = [eye_w + x for x in lab]
    x = [v.astype(BF16) for v in lab]
    x = [mm(v, blk(v)).astype(BF16) for v in x]
    for _ in range(4):
        tx = [mm(stack(tv.astype(BF16), xv), blk(xv)) for tv, xv in zip(t, x)]
        t = [tv + u[:c] for tv, u in zip(t, tx)]
        x = [u[c:].astype(BF16) for u in tx]
    t = [tv + mm(tv.astype(BF16), blk(xv)) for tv, xv in zip(t, x)]
    lmkv = [mm(l_, v) for l_, v in zip(lmk, vblk)]
    mt = [mm(stack(m_, u[:, :gw]), blk(tv.astype(BF16))).astype(BF16) for m_, u, tv in zip(mrb, bkt, t)]
    ryg = [mm(w, side(blk(a), blk(u[:c].astype(BF16)))) for w, a, u in zip(mt, a_t, lmkv)]
    r_hat = [rf + u[:c, :gw] for rf, u in zip(r_f, ryg)]
    y_hat = [u[:c, gw:] + w[c:2 * c] for u, w in zip(ryg, lmkv)]
    g_w = [eye_w * we + u[c:, :gw] for we, u in zip(w_end, ryg)]
    z_w = [u[c:, gw:] + w[2 * c:] for u, w in zip(ryg, lmkv)]
    lhs = [stack(rh, gv).astype(BF16) for rh, gv in zip(r_hat, g_w)]
    state = [h_ref[:, sl] for sl in groups]
    for ci in range(n_sub):
        for gi, sl in enumerate(groups):
            i = ci * len(groups) + gi
            u = mm(lhs[i], blk(state[gi].astype(BF16)))
            y_ref[ci * c:(ci + 1) * c, sl] = u[:c] + y_hat[i]
            state[gi] = u[c:] + z_w[i]
    for gi, sl in enumerate(groups):
        h_ref[:, sl] = state[gi]


def _dot_hilo_left(a, b):
    hi = b.astype(BF16)
    lo = (b - hi.astype(F32)).astype(BF16)
    return jnp.dot(a, hi, preferred_element_type=F32) + jnp.dot(a, lo, preferred_element_type=F32)


def _wkv(lw, r, k, v, kk, b, blkmask, batch, seq):
    m, d = r.shape
    rows = WKV_CHUNK * WKV_SUBCHUNKS
    nc = seq // rows
    spec = pl.BlockSpec((rows, d), lambda bi, ci: (bi * nc + ci, 0))
    return pl.pallas_call(
        functools.partial(_wkv_body, n_sub=WKV_SUBCHUNKS), grid=(batch, nc),
        in_specs=[spec] * 6 + [_resident(blkmask.shape)], out_specs=spec,
        out_shape=jax.ShapeDtypeStruct((m, d), F32),
        scratch_shapes=[pltpu.VMEM((RWKV_HEAD, d), F32)],
        compiler_params=_params("parallel", "arbitrary"), name="wkv")(lw, r, k, v, kk, b, blkmask)


def _rwkv_out_body(y_ref, bonus_ref, gate_ref, h_ref, gnw_ref, gnb_ref, wo_ref, e_ref, et2_ref, out_ref):
    e, et2 = e_ref[...], et2_ref[...]
    y = y_ref[...]
    inv_n = 1.0 / RWKV_HEAD
    mu = _bcast_hilo(_dot(y, e) * inv_n, et2)
    dlt = y - mu
    var = _dot(dlt * dlt, e) * inv_n
    rstd = _bcast_hilo(lax.rsqrt(var + GN_EPS), et2)
    yn = dlt * rstd * gnw_ref[...] + gnb_ref[...]
    out = (yn + bonus_ref[...].astype(F32)) * gate_ref[...].astype(F32)
    out_ref[...] = h_ref[...] + _dot(out, wo_ref[...])


def _rwkv_out(y, bonus, gate, h, p):
    m, d = h.shape
    tm = LIGHT_TILE
    vec = _resident((1, d))
    return pl.pallas_call(
        _rwkv_out_body, grid=(m // tm,),
        in_specs=[_rows(tm, d)] * 4 + [vec, vec, _resident((d, d)), _resident(p["e"].shape), _resident(p["et2"].shape)],
        out_specs=_rows(tm, d), out_shape=jax.ShapeDtypeStruct((m, d), F32),
        compiler_params=_params("parallel"), name="rwkv_out")(
        y, bonus, gate, h, p["gn_w"], p["gn_b"], p["wo"], p["e"], p["et2"])


def _rope_body(pos_ref, freq_ref, sign_ref, cos_ref, sin_ref):
    ang = pos_ref[...].astype(F32) * freq_ref[...]
    cos_ref[...] = jnp.cos(ang)
    sin_ref[...] = jnp.sin(ang) * sign_ref[...]


def _rope_tables(pos, freq, sign):
    m = pos.shape[0]
    tm = LIGHT_TILE
    out = jax.ShapeDtypeStruct((m, LANES), F32)
    row = _resident((1, LANES))
    return pl.pallas_call(
        _rope_body, grid=(m // tm,),
        in_specs=[_rows(tm, 1), row, row],
        out_specs=[_rows(tm, LANES)] * 2, out_shape=[out, out],
        compiler_params=_params("parallel"), name="rope_tables")(pos, freq, sign)


def _mla_kv_body(h_ref, g_ref, wd_ref, lg_ref, wuk_ref, wuvt_ref, cos_ref, sin_ref, kn_out, kr_out, vt_out):
    hn = _rms(h_ref[...], g_ref[...])
    ckv = _dot(hn, wd_ref[...])
    lat = _rms(ckv[:, :KV_LORA], lg_ref[...]).astype(BF16)
    kn_out[...] = jnp.dot(lat, wuk_ref[...], preferred_element_type=F32).astype(BF16)
    vt_out[...] = _dot_nt(wuvt_ref[...], lat).astype(BF16)
    kr = ckv[:, KV_LORA:KV_LORA + LANES]
    krs = ckv[:, KV_LORA + LANES:]
    kr_out[...] = (kr * cos_ref[...] + krs * sin_ref[...]).astype(BF16)


def _mla_kv(h, p, cos, sin):
    m, d = h.shape
    tm = LIGHT_TILE
    hv = MLA_HEADS * V_HEAD
    return pl.pallas_call(
        _mla_kv_body, grid=(m // tm,),
        in_specs=[_rows(tm, d), _resident((1, d)), _resident(p["w_dkv"].shape), _resident((1, KV_LORA)),
                  _resident(p["w_uk"].shape), _resident(p["w_uv_t"].shape), _rows(tm, LANES), _rows(tm, LANES)],
        out_specs=[_rows(tm, hv), _rows(tm, LANES), pl.BlockSpec((hv, tm), lambda i: (0, i))],
        out_shape=[jax.ShapeDtypeStruct((m, hv), BF16), jax.ShapeDtypeStruct((m, LANES), BF16),
                   jax.ShapeDtypeStruct((hv, m), BF16)],
        compiler_params=_params("parallel"), name="mla_kv")(
        h, p["kv_norm_g"], p["w_dkv"], p["kv_latent_g"], p["w_uk"], p["w_uv_t"], cos, sin)


def _mla_q_body(h_ref, g_ref, wd_ref, lg_ref, wu_ref, cos_ref, sin_ref, qn_out, qr_out, *, scale):
    hn = _rms(h_ref[...], g_ref[...])
    lat = _rms(_dot(hn, wd_ref[...]), lg_ref[...])
    q = _dot(lat, wu_ref[...]) * scale
    w = q.shape[1] // 3
    cos = jnp.concatenate([cos_ref[...]] * MLA_HEADS, axis=1)
    sin = jnp.concatenate([sin_ref[...]] * MLA_HEADS, axis=1)
    qn_out[...] = q[:, :w].astype(BF16)
    qr_out[...] = (q[:, w:2 * w] * cos + q[:, 2 * w:] * sin).astype(BF16)


def _mla_q(h, g, p, cos, sin):
    m, d = h.shape
    tm = LIGHT_TILE
    w = MLA_HEADS * LANES
    scale = float(QK_NOPE + QK_ROPE) ** -0.5 * float(np.log2(np.e))
    out = jax.ShapeDtypeStruct((m, w), BF16)
    return pl.pallas_call(
        functools.partial(_mla_q_body, scale=scale), grid=(m // tm,),
        in_specs=[_rows(tm, d), _resident((1, d)), _resident(p["w_dq"].shape), _resident(p["q_latent_g"].shape),
                  _resident(p["w_uq"].shape), _rows(tm, LANES), _rows(tm, LANES)],
        out_specs=[_rows(tm, w)] * 2, out_shape=[out, out],
        compiler_params=_params("parallel"), name="mla_q")(
        h, g, p["w_dq"], p["q_latent_g"], p["w_uq"], cos, sin)


def _attn_body(qn_ref, qr_ref, kn_ref, kr_ref, vt_ref, o_ref):
    t = ATTN_TILE
    rows = ATTN_CHAIN
    n_chain = t // rows
    n_tiles = qn_ref.shape[0] // t

    def queries(qt, i):
        lo = qt * t + i * rows
        return jnp.concatenate([qn_ref[lo:lo + rows, :], qr_ref[lo:lo + rows, :]], axis=1)

    def keys(kt, n):
        return jnp.concatenate([kn_ref[kt * t:kt * t + n, :], kr_ref[kt * t:kt * t + n, :]], axis=1)

    def n_keys(qt, kt, i):
        return t if kt < qt else (i + 1) * rows

    def scores(qt, kt, i):
        n = n_keys(qt, kt, i)
        s_t = _dot_nt(keys(kt, n), queries(qt, i))
        if kt == qt:
            mask = (lax.broadcasted_iota(jnp.int32, (n, rows), 1) + i * rows
                    >= lax.broadcasted_iota(jnp.int32, (n, rows), 0))
            s_t = jnp.where(mask, s_t, -1e30)
        return s_t

    def update(carry, s_t, v_t):
        m_i, l_i, acc = carry
        m_new = jnp.maximum(m_i, jnp.max(s_t, axis=0, keepdims=True))
        p_t = jnp.exp2(s_t - m_new)
        alpha = jnp.exp2(m_i - m_new)
        l_new = alpha * l_i + jnp.sum(p_t, axis=0, keepdims=True)
        return m_new, l_new, alpha * acc + jnp.dot(v_t, p_t.astype(BF16), preferred_element_type=F32)

    steps = [(qt, kt, i) for qt in range(n_tiles) for kt in range(qt + 1) for i in range(n_chain)]
    one = (jnp.full((1, rows), -jnp.inf, F32), jnp.zeros((1, rows), F32), jnp.zeros((V_HEAD, rows), F32))
    state = [[one] * n_chain for _ in range(n_tiles)]
    ahead = ATTN_LOOKAHEAD
    s_buf = {n: scores(*steps[n]) for n in range(min(ahead, len(steps)))}
    for n, (qt, kt, i) in enumerate(steps):
        if n + ahead < len(steps):
            s_buf[n + ahead] = scores(*steps[n + ahead])
        nk = n_keys(qt, kt, i)
        state[qt][i] = update(state[qt][i], s_buf.pop(n), vt_ref[:, kt * t:kt * t + nk])
        if kt == qt:
            _, l_i, acc = state[qt][i]
            lo = qt * t + i * rows
            o_ref[lo:lo + rows, :] = (acc / l_i).T.astype(BF16)


def _attention(qn, qr, kn, kr, vt, batch, seq):
    m = qn.shape[0]
    spec = pl.BlockSpec((seq, LANES), lambda b, h: (b, h))
    krspec = pl.BlockSpec((seq, LANES), lambda b, h: (b, 0))
    vtspec = pl.BlockSpec((V_HEAD, seq), lambda b, h: (h, b))
    return pl.pallas_call(
        _attn_body, grid=(batch, MLA_HEADS),
        in_specs=[spec, spec, spec, krspec, vtspec], out_specs=spec,
        out_shape=jax.ShapeDtypeStruct((m, MLA_HEADS * V_HEAD), BF16),
        compiler_params=_params("parallel", "parallel"), name="mla_attention")(qn, qr, kn, kr, vt)


def _prep_rwkv(mix, w_r, w_k, w_v, w_o, w0, w1, w2, a0, a1, a2, g1, g2, k_k, k_a, r_k, gn_w, gn_b):
    d = w_r.shape[0]
    assert (w1.shape[1], a1.shape[1], g1.shape[1]) == (DECAY_LORA, AAA_LORA, GATE_LORA)
    head_of = np.arange(d) // RWKV_HEAD
    e = (head_of[:, None] == np.arange(LANES)[None, :]).astype(np.float32)
    first = jnp.concatenate([w1, a1, g1], axis=1)
    shift = jnp.concatenate([mix[1][:, None] * w1, mix[4][:, None] * a1, mix[5][:, None] * g1], axis=1)
    wl1 = jnp.concatenate([first, shift], axis=0)
    zeros = lambda n: jnp.zeros((n, d), w2.dtype)
    wl2 = jnp.concatenate([
        jnp.concatenate([w2, zeros(AAA_LORA), zeros(GATE_LORA)], axis=0),
        jnp.concatenate([zeros(DECAY_LORA), a2, zeros(GATE_LORA)], axis=0),
        jnp.concatenate([zeros(DECAY_LORA), zeros(AAA_LORA), g2], axis=0)], axis=1)
    return dict(
        mix=mix, wr=w_r.astype(BF16), wk=w_k.astype(BF16), wv=w_v.astype(BF16), wo=w_o.astype(BF16),
        wl1=wl1.astype(BF16), wl2=wl2.astype(BF16), w0=w0[None, :], a0=a0[None, :],
        k_k=k_k[None, :], k_a=k_a[None, :], r_k=r_k.reshape(1, d),
        gn_w=gn_w[None, :], gn_b=gn_b[None, :], e=jnp.asarray(e, BF16), et=jnp.asarray(e.T, BF16),
        et2=jnp.asarray(np.concatenate([e.T, e.T], axis=0), BF16))


def _prep_mla(kv_norm_g, w_dkv, kv_latent_g, w_ukv, w_dq, q_latent_g, w_uq, w_o):
    half = QK_ROPE // 2
    zpad = LANES - QK_ROPE
    x1 = w_dkv[:, KV_LORA:KV_LORA + half]
    x2 = w_dkv[:, KV_LORA + half:]
    z = jnp.zeros((w_dkv.shape[0], zpad), w_dkv.dtype)
    w_dkv_p = jnp.concatenate([w_dkv[:, :KV_LORA], x1, x2, z, x2, x1, z], axis=1)
    ukv = w_ukv.reshape(KV_LORA, MLA_HEADS, QK_NOPE + V_HEAD)
    w_uk = ukv[:, :, :QK_NOPE].reshape(KV_LORA, -1)
    w_uv_t = ukv[:, :, QK_NOPE:].reshape(KV_LORA, -1).T
    uq = w_uq.reshape(w_uq.shape[0], MLA_HEADS, QK_NOPE + QK_ROPE)
    q1 = uq[:, :, QK_NOPE:QK_NOPE + half]
    q2 = uq[:, :, QK_NOPE + half:]
    zq = jnp.zeros(q1.shape[:2] + (zpad,), w_uq.dtype)
    flat = lambda t: t.reshape(t.shape[0], -1)
    w_uq_p = jnp.concatenate([flat(uq[:, :, :QK_NOPE]), flat(jnp.concatenate([q1, q2, zq], axis=2)),
                              flat(jnp.concatenate([q2, q1, zq], axis=2))], axis=1)
    return dict(kv_norm_g=kv_norm_g[None, :], w_dkv=w_dkv_p.astype(BF16), kv_latent_g=kv_latent_g[None, :],
                w_uk=w_uk.astype(BF16), w_uv_t=w_uv_t.astype(BF16), w_dq=w_dq.astype(BF16), q_latent_g=q_latent_g[None, :],
                w_uq=w_uq_p.astype(BF16), w_o=w_o.astype(BF16))


def _rope_consts():
    half = QK_ROPE // 2
    inv_freq = ROPE_THETA ** (-jnp.arange(0, QK_ROPE, 2, dtype=F32) / QK_ROPE)
    reps = LANES // half
    freq = jnp.tile(inv_freq, reps)[None, :]
    pair = lambda a, b: jnp.tile(jnp.concatenate([jnp.full((half,), a, F32), jnp.full((half,), b, F32)]), reps // 2)[None, :]
    return freq, pair(-1.0, 1.0)


def kernel(x, positions, norm_g, ffn_w_gate, ffn_w_up, ffn_w_down, rwkv_mix, rwkv_w_r, rwkv_w_k, rwkv_w_v, rwkv_w_o, rwkv_w0, rwkv_w1, rwkv_w2, rwkv_a0, rwkv_a1, rwkv_a2, rwkv_g1, rwkv_g2, rwkv_k_k, rwkv_k_a, rwkv_r_k, rwkv_gn_w, rwkv_gn_b, kv_norm_g, mla_w_dkv, mla_kv_latent_g, mla_w_ukv, mla_w_dq, mla_q_latent_g, mla_w_uq, mla_w_o, final_norm_g):
    batch, seq, d = x.shape
    depth = norm_g.shape[0]
    n_a = rwkv_mix.shape[0]
    m = batch * seq
    assert all(seq % t == 0 for t in (TOKEN_TILE, LIGHT_TILE, FFN_TILE, ATTN_TILE, WKV_CHUNK * WKV_SUBCHUNKS))
    assert d % MXU_DIM == 0

    wg, wu, wd = ffn_w_gate.astype(BF16), ffn_w_up.astype(BF16), ffn_w_down.astype(BF16)
    blkmask = jnp.asarray(np.kron(np.eye(WKV_GROUP), np.ones((RWKV_HEAD, RWKV_HEAD))), BF16)
    cos, sin = _rope_tables(positions.reshape(m, 1), *_rope_consts())
    mla = None

    h = x.reshape(m, d)
    kv = None
    for layer in range(depth):
        norm = lambda j: norm_g[layer, j][None, :]
        if layer == n_a:
            mla = _prep_mla(kv_norm_g, mla_w_dkv, mla_kv_latent_g, mla_w_ukv, mla_w_dq[0], mla_q_latent_g[0],
                            mla_w_uq[0], mla_w_o[0])
            kv = _mla_kv(h, mla, cos, sin)
        h = _ffn(h, norm(0), wg, wu, wd, (layer, 0))
        last = layer == depth - 1
        if layer < n_a:
            i = layer
            p = _prep_rwkv(rwkv_mix[i], rwkv_w_r[i], rwkv_w_k[i], rwkv_w_v[i], rwkv_w_o[i], rwkv_w0[i], rwkv_w1[i],
                           rwkv_w2[i], rwkv_a0[i], rwkv_a1[i], rwkv_a2[i], rwkv_g1[i], rwkv_g2[i], rwkv_k_k[i],
                           rwkv_k_a[i], rwkv_r_k[i], rwkv_gn_w[i], rwkv_gn_b[i])
            r, k, v, kk, b, lw, gate, bonus = _rwkv_proj(h, norm(1), p, seq)
            y = _wkv(lw, r, k, v, kk, b, blkmask, batch, seq)
            h = _rwkv_out(y, bonus, gate, h, p)
            pre = None
        else:
            j = layer - n_a
            if j > 0:
                mla = dict(mla, **{n: v_ for n, v_ in _prep_mla(
                    kv_norm_g, mla_w_dkv, mla_kv_latent_g, mla_w_ukv, mla_w_dq[j], mla_q_latent_g[j], mla_w_uq[j],
                    mla_w_o[j]).items() if n in ("w_dq", "q_latent_g", "w_uq", "w_o")})
            qn, qr = _mla_q(h, norm(1), mla, cos, sin)
            o = _attention(qn, qr, kv[0], kv[1], kv[2], batch, seq)
            pre = (o, mla["w_o"])
        h = _ffn(h, norm(2), wg, wu, wd, (layer, 1), pre=pre,
                 final_g=final_norm_g[None, :] if last else None)
    return h.reshape(batch, seq, d)
```

```python
import functools

import numpy as np
import jax
import jax.numpy as jnp
from jax import lax
from jax.experimental import pallas as pl
from jax.experimental.pallas import tpu as pltpu

F32, BF16 = jnp.float32, jnp.bfloat16

RWKV_HEAD = 64
DECAY_LORA = 64
AAA_LORA = 64
GATE_LORA = 128
GN_EPS = 64e-5
RMS_EPS = 1e-6
MLA_HEADS = 8
QK_NOPE = 128
QK_ROPE = 64
V_HEAD = 128
KV_LORA = 256
ROPE_THETA = 10000.0

LANES = 128
MXU_DIM = 256
VMEM_LIMIT = 56 * 1024 * 1024

TOKEN_TILE = 512
LIGHT_TILE = 1024
PROJ_ROW_GROUPS = 2
FFN_TILE = 1024
FFN_ROW_GROUPS = 4
WKV_CHUNK = 64
WKV_GROUP = MXU_DIM // RWKV_HEAD
WKV_SUBCHUNKS = 8
ATTN_TILE = 1024
ATTN_CHAIN = 256
ATTN_LOOKAHEAD = 6


def _params(*sem):
    return pltpu.CompilerParams(dimension_semantics=sem, vmem_limit_bytes=VMEM_LIMIT)


def _resident(shape):
    return pl.BlockSpec(shape, lambda *_: (0,) * len(shape), pipeline_mode=pl.Buffered(1))


def _rows(tm, d):
    return pl.BlockSpec((tm, d), lambda i: (i, 0))


def _dot(a, b):
    return jnp.dot(a.astype(BF16), b.astype(BF16), preferred_element_type=F32)


def _dot_nt(a, b):
    return lax.dot_general(a.astype(BF16), b.astype(BF16), (((1,), (1,)), ((), ())),
                           preferred_element_type=F32)


def _dot_hilo(a, b):
    hi = a.astype(BF16)
    lo = (a - hi.astype(F32)).astype(BF16)
    return (jnp.dot(hi, b, preferred_element_type=F32) + jnp.dot(lo, b, preferred_element_type=F32))


def _rms(x, g):
    return x * lax.rsqrt(jnp.mean(x * x, axis=-1, keepdims=True) + RMS_EPS) * g


def _ffn_body(*refs, n_chunks, pre_proj, final):
    refs = list(refs)
    x_ref = refs.pop(0)
    if pre_proj:
        o_ref_in, wo_ref = refs.pop(0), refs.pop(0)
    g_ref, wg_ref, wu_ref, wd_ref = refs[:4]
    out_ref = refs[-1]
    tiles = wg_ref.shape[1] // MXU_DIM
    bounds = [MXU_DIM * (tiles * c // n_chunks) for c in range(n_chunks + 1)]
    chunks = [slice(lo, hi) for lo, hi in zip(bounds, bounds[1:])]
    rows = x_ref.shape[0] // FFN_ROW_GROUPS
    groups = [slice(r * rows, (r + 1) * rows) for r in range(FFN_ROW_GROUPS)]
    xs = []
    for rs in groups:
        x = x_ref[rs, :]
        if pre_proj:
            x = x + jnp.dot(o_ref_in[rs, :], wo_ref[...], preferred_element_type=F32)
        xs.append(x)
    xn = [_rms(x, g_ref[...]).astype(BF16) for x in xs]
    gu = [[(jnp.dot(v, wg_ref[:, sl], preferred_element_type=F32),
            jnp.dot(v, wu_ref[:, sl], preferred_element_type=F32)) for sl in chunks] for v in xn]
    for rs, x, gu_r in zip(groups, xs, gu):
        acts = [(gate * jax.nn.sigmoid(gate) * up).astype(BF16) for gate, up in gu_r]
        acc = sum(jnp.dot(a, wd_ref[sl, :], preferred_element_type=F32) for a, sl in zip(acts, chunks))
        y = x + 0.5 * acc
        if final:
            y = _rms(y, refs[4][...])
        out_ref[rs, :] = y


def _ffn(x, g, wg, wu, wd, which, *, pre=None, final_g=None):
    m, d = x.shape
    f = wg.shape[-1]
    tm = FFN_TILE
    ins, specs = [x], [_rows(tm, d)]
    if pre is not None:
        o, wo = pre
        ins += [o, wo]
        specs += [_rows(tm, o.shape[1]), _resident(wo.shape)]
    picked = lambda r, c: pl.BlockSpec((None, None, r, c), lambda i: (*which, 0, 0), pipeline_mode=pl.Buffered(1))
    ins += [g, wg, wu, wd]
    specs += [_resident((1, d)), picked(d, f), picked(d, f), picked(f, d)]
    if final_g is not None:
        ins.append(final_g)
        specs.append(_resident((1, d)))
    body = functools.partial(_ffn_body, n_chunks=3, pre_proj=pre is not None, final=final_g is not None)
    return pl.pallas_call(
        body, grid=(m // tm,), in_specs=specs, out_specs=_rows(tm, d),
        out_shape=jax.ShapeDtypeStruct((m, d), F32), compiler_params=_params("parallel"),
        name="ffn")(*ins)


def _softplus(z):
    return jnp.maximum(z, 0.0) + jnp.log(1.0 + jnp.exp(-jnp.abs(z)))


def _bcast_hilo(s, et2):
    hi = s.astype(BF16)
    lo = (s - hi.astype(F32)).astype(BF16)
    return jnp.dot(jnp.concatenate([hi, lo], axis=1), et2, preferred_element_type=F32)


def _rwkv_proj_body(h_ref, hp_ref, g_ref, mix_ref, wr_ref, wk_ref, wv_ref, wl1_ref, wl2_ref, w0_ref, a0_ref,
                    kk_ref, ka_ref, rk_ref, e_ref, et_ref, et2_ref,
                    r_out, k_out, v_out, kkn_out, b_out, lw_out, gate_out, bonus_out, *, tiles_per_seq):
    i = pl.program_id(0)
    gn = g_ref[...]
    hn = _rms(h_ref[...], gn)
    tm, d = hn.shape
    prev = _rms(hp_ref[...], gn)[7:8, :]
    prev = jnp.where(i % tiles_per_seq == 0, 0.0, prev)
    shifted = pltpu.roll(hn, 1, 0)
    row = lax.broadcasted_iota(jnp.int32, (tm, 1), 0)
    shifted = jnp.where(row == 0, prev, shifted)
    xx = shifted - hn
    e, et = e_ref[...], et_ref[...]
    lane = lax.broadcasted_iota(jnp.int32, (1, wl1_ref.shape[1]), 1)

    def project(rs):
        hn_g, xx_g = hn[rs], xx[rs]
        xr, xk, xv = (hn_g + xx_g * mix_ref[c:c + 1, :] for c in (0, 2, 3))
        r = _dot(xr, wr_ref[...])
        k = _dot(xk, wk_ref[...])
        v = _dot(xv, wv_ref[...])
        l1 = jnp.dot(jnp.concatenate([hn_g, xx_g], axis=1).astype(BF16), wl1_ref[...], preferred_element_type=F32)
        act = jnp.where(lane < DECAY_LORA, jnp.tanh(l1),
                        jnp.where(lane < DECAY_LORA + AAA_LORA, l1, jax.nn.sigmoid(l1)))
        return r, k, v, _dot(act, wl2_ref[...])

    def finish(rs, r, k, v, l2):
        w_log = -_softplus(-(w0_ref[...] + l2[:, :d])) - 0.5
        lw_out[rs, :] = -jnp.exp(w_log)
        a = jax.nn.sigmoid(a0_ref[...] + l2[:, d:2 * d])
        gate_out[rs, :] = l2[:, 2 * d:].astype(BF16)
        kk = k * kk_ref[...]
        inv = lax.rsqrt(jnp.maximum(_dot(kk * kk, e), 1e-24))
        kk = kk * _bcast_hilo(inv, et2_ref[...])
        k = k * (1.0 + (a - 1.0) * ka_ref[...])
        r_out[rs, :] = r.astype(BF16)
        k_out[rs, :] = k.astype(BF16)
        v_out[rs, :] = v.astype(BF16)
        kkn_out[rs, :] = kk.astype(BF16)
        b_out[rs, :] = (kk * a).astype(BF16)
        bonus_out[rs, :] = (_dot(_dot(r * k * rk_ref[...], e), et) * v).astype(BF16)

    rows = tm // PROJ_ROW_GROUPS
    groups = [slice(g_ * rows, (g_ + 1) * rows) for g_ in range(PROJ_ROW_GROUPS)]
    projected = [project(rs) for rs in groups]
    for rs, vals in zip(groups, projected):
        finish(rs, *vals)


def _rwkv_proj(h, g, p, seq):
    m, d = h.shape
    tm = TOKEN_TILE
    vec = _resident((1, d))
    in_specs = ([_rows(tm, d), pl.BlockSpec((8, d), lambda i: (jnp.maximum(i * (tm // 8) - 1, 0), 0)),
                 vec, _resident((6, d)), _resident((d, d)), _resident((d, d)), _resident((d, d)),
                 _resident(p["wl1"].shape), _resident(p["wl2"].shape), vec, vec, vec, vec, vec,
                 _resident(p["e"].shape), _resident(p["et"].shape), _resident(p["et2"].shape)])
    f32o = jax.ShapeDtypeStruct((m, d), F32)
    bf16o = jax.ShapeDtypeStruct((m, d), BF16)
    body = functools.partial(_rwkv_proj_body, tiles_per_seq=seq // tm)
    return pl.pallas_call(
        body, grid=(m // tm,), in_specs=in_specs, out_specs=[_rows(tm, d)] * 8,
        out_shape=[bf16o, bf16o, bf16o, bf16o, bf16o, f32o, bf16o, bf16o],
        compiler_params=_params("parallel"), name="rwkv_proj")(
        h, h, g, p["mix"], p["wr"], p["wk"], p["wv"], p["wl1"], p["wl2"], p["w0"], p["a0"],
        p["k_k"], p["k_a"], p["r_k"], p["e"], p["et"], p["et2"])


def _blk(x, blkmask):
    return jnp.concatenate([x] * WKV_GROUP, axis=0) * blkmask


def _wkv_body(lw_ref, r_ref, k_ref, v_ref, kk_ref, b_ref, blk_ref, y_ref, h_ref, *, n_sub):
    c = WKV_CHUNK
    gw = MXU_DIM

    @pl.when(pl.program_id(1) == 0)
    def _():
        h_ref[...] = jnp.zeros_like(h_ref)

    blkmask = blk_ref[...]
    row = lax.broadcasted_iota(jnp.int32, (c, gw), 0)
    colj = lax.broadcasted_iota(jnp.int32, (c, gw), 1) % c
    strict, incl = row > colj, row >= colj
    eye_w = (row == colj).astype(F32)
    eye_b = eye_w.astype(BF16)
    r2 = lax.broadcasted_iota(jnp.int32, (c, c), 0)
    c2 = lax.broadcasted_iota(jnp.int32, (c, c), 1)
    tri = (r2 >= c2).astype(BF16)

    blk = lambda x: _blk(x, blkmask)
    mm = lambda a, b: jnp.dot(a, b, preferred_element_type=F32)
    stack = lambda *a: jnp.concatenate(a, axis=0)
    side = lambda *a: jnp.concatenate(a, axis=1)
    groups = [slice(g * gw, (g + 1) * gw) for g in range(h_ref.shape[1] // gw)]

    a_t, r_f, r_t, b_t, k_t, b_c, k_c, vg, w_end = ([] for _ in range(9))
    for ci in range(n_sub):
        rows = slice(ci * c, (ci + 1) * c)
        lw = lw_ref[rows, :]
        cs = _dot_hilo_left(tri, lw)
        cs_end = cs[c - 1:c, :]
        w_inv = jnp.exp(-cs)
        w_rem = jnp.exp(cs_end - cs)
        a_all = (-kk_ref[rows, :].astype(F32) * jnp.exp(cs - lw)).astype(BF16)
        r_all = r_ref[rows, :].astype(F32) * jnp.exp(cs)
        b_raw, k_raw, v_all = b_ref[rows, :].astype(F32), k_ref[rows, :].astype(F32), v_ref[rows, :]
        bt_all, kt_all = (b_raw * w_inv).astype(BF16), (k_raw * w_inv).astype(BF16)
        bc_all, kc_all = (b_raw * w_rem).astype(BF16), (k_raw * w_rem).astype(BF16)
        we_all = jnp.exp(cs_end)
        for sl in groups:
            a_t.append(a_all[:, sl])
            r_f.append(r_all[:, sl])
            r_t.append(r_all[:, sl].astype(BF16))
            b_t.append(bt_all[:, sl])
            k_t.append(kt_all[:, sl])
            b_c.append(bc_all[:, sl])
            k_c.append(kc_all[:, sl])
            vg.append(v_all[:, sl])
            w_end.append(we_all[:, sl])

    vblk = [blk(v) for v in vg]
    ar = [stack(a, r) for a, r in zip(a_t, r_t)]
    sb = [_dot_nt(x, blk(b)) for x, b in zip(ar, b_t)]
    sk = [_dot_nt(x, blk(k)) for x, k in zip(ar, k_t)]
    bkt = [_dot_nt(eye_b, stack(blk(b), blk(k))).astype(BF16) for b, k in zip(b_c, k_c)]
    lab = [jnp.where(strict, s[:c], 0.0) for s in sb]
    mrb = [jnp.where(incl, s[c:], 0.0).astype(BF16) for s in sb]
    lmk = [stack(jnp.where(strict, s[:c], 0.0).astype(BF16), jnp.where(incl, s[c:], 0.0).astype(BF16), u[:, gw:])
           for s, u in zip(sk, bkt)]
    t = [eye_w + x for x in lab]
    x = [v.astype(BF16) for v in lab]
    x = [mm(v, blk(v)).astype(BF16) for v in x]
    for _ in range(4):
        tx = [mm(stack(tv.astype(BF16), xv), blk(xv)) for tv, xv in zip(t, x)]
        t = [tv + u[:c] for tv, u in zip(t, tx)]
        x = [u[c:].astype(BF16) for u in tx]
    t = [tv + mm(tv.astype(BF16), blk(xv)) for tv, xv in zip(t, x)]
    lmkv = [mm(l_, v) for l_, v in zip(lmk, vblk)]
    mt = [mm(stack(m_, u[:, :gw]), blk(tv.astype(BF16))).astype(BF16) for m_, u, tv in zip(mrb, bkt, t)]
    ryg = [mm(w, side(blk(a), blk(u[:c].astype(BF16)))) for w, a, u in zip(mt, a_t, lmkv)]
    r_hat = [rf + u[:c, :gw] for rf, u in zip(r_f, ryg)]
    y_hat = [u[:c, gw:] + w[c:2 * c] for u, w in zip(ryg, lmkv)]
    g_w = [eye_w * we + u[c:, :gw] for we, u in zip(w_end, ryg)]
    z_w = [u[c:, gw:] + w[2 * c:] for u, w in zip(ryg, lmkv)]
    lhs = [stack(rh, gv).astype(BF16) for rh, gv in zip(r_hat, g_w)]
    state = [h_ref[:, sl] for sl in groups]
    for ci in range(n_sub):
        for gi, sl in enumerate(groups):
            i = ci * len(groups) + gi
            u = mm(lhs[i], blk(state[gi].astype(BF16)))
            y_ref[ci * c:(ci + 1) * c, sl] = u[:c] + y_hat[i]
            state[gi] = u[c:] + z_w[i]
    for gi, sl in enumerate(groups):
        h_ref[:, sl] = state[gi]


def _dot_hilo_left(a, b):
    hi = b.astype(BF16)
    lo = (b - hi.astype(F32)).astype(BF16)
    return jnp.dot(a, hi, preferred_element_type=F32) + jnp.dot(a, lo, preferred_element_type=F32)


def _wkv(lw, r, k, v, kk, b, blkmask, batch, seq):
    m, d = r.shape
    rows = WKV_CHUNK * WKV_SUBCHUNKS
    nc = seq // rows
    spec = pl.BlockSpec((rows, d), lambda bi, ci: (bi * nc + ci, 0))
    return pl.pallas_call(
        functools.partial(_wkv_body, n_sub=WKV_SUBCHUNKS), grid=(batch, nc),
        in_specs=[spec] * 6 + [_resident(blkmask.shape)], out_specs=spec,
        out_shape=jax.ShapeDtypeStruct((m, d), F32),
        scratch_shapes=[pltpu.VMEM((RWKV_HEAD, d), F32)],
        compiler_params=_params("parallel", "arbitrary"), name="wkv")(lw, r, k, v, kk, b, blkmask)


def _rwkv_out_body(y_ref, bonus_ref, gate_ref, h_ref, gnw_ref, gnb_ref, wo_ref, e_ref, et2_ref, out_ref):
    e, et2 = e_ref[...], et2_ref[...]
    y = y_ref[...]
    inv_n = 1.0 / RWKV_HEAD
    mu = _bcast_hilo(_dot(y, e) * inv_n, et2)
    dlt = y - mu
    var = _dot(dlt * dlt, e) * inv_n
    rstd = _bcast_hilo(lax.rsqrt(var + GN_EPS), et2)
    yn = dlt * rstd * gnw_ref[...] + gnb_ref[...]
    out = (yn + bonus_ref[...].astype(F32)) * gate_ref[...].astype(F32)
    out_ref[...] = h_ref[...] + _dot(out, wo_ref[...])


def _rwkv_out(y, bonus, gate, h, p):
    m, d = h.shape
    tm = LIGHT_TILE
    vec = _resident((1, d))
    return pl.pallas_call(
        _rwkv_out_body, grid=(m // tm,),
        in_specs=[_rows(tm, d)] * 4 + [vec, vec, _resident((d, d)), _resident(p["e"].shape), _resident(p["et2"].shape)],
        out_specs=_rows(tm, d), out_shape=jax.ShapeDtypeStruct((m, d), F32),
        compiler_params=_params("parallel"), name="rwkv_out")(
        y, bonus, gate, h, p["gn_w"], p["gn_b"], p["wo"], p["e"], p["et2"])


def _rope_body(pos_ref, freq_ref, sign_ref, cos_ref, sin_ref):
    ang = pos_ref[...].astype(F32) * freq_ref[...]
    cos_ref[...] = jnp.cos(ang)
    sin_ref[...] = jnp.sin(ang) * sign_ref[...]


def _rope_tables(pos, freq, sign):
    m = pos.shape[0]
    tm = LIGHT_TILE
    out = jax.ShapeDtypeStruct((m, LANES), F32)
    row = _resident((1, LANES))
    return pl.pallas_call(
        _rope_body, grid=(m // tm,),
        in_specs=[_rows(tm, 1), row, row],
        out_specs=[_rows(tm, LANES)] * 2, out_shape=[out, out],
        compiler_params=_params("parallel"), name="rope_tables")(pos, freq, sign)


def _mla_kv_body(h_ref, g_ref, wd_ref, lg_ref, wuk_ref, wuvt_ref, cos_ref, sin_ref, kn_out, kr_out, vt_out):
    hn = _rms(h_ref[...], g_ref[...])
    ckv = _dot(hn, wd_ref[...])
    lat = _rms(ckv[:, :KV_LORA], lg_ref[...]).astype(BF16)
    kn_out[...] = jnp.dot(lat, wuk_ref[...], preferred_element_type=F32).astype(BF16)
    vt_out[...] = _dot_nt(wuvt_ref[...], lat).astype(BF16)
    kr = ckv[:, KV_LORA:KV_LORA + LANES]
    krs = ckv[:, KV_LORA + LANES:]
    rot = kr * cos_ref[...] + krs * sin_ref[...]
    kr_out[...] = jnp.concatenate([rot, pltpu.roll(rot, QK_ROPE, 1)], axis=1).astype(BF16)


def _mla_kv(h, p, cos, sin):
    m, d = h.shape
    tm = LIGHT_TILE
    hv = MLA_HEADS * V_HEAD
    return pl.pallas_call(
        _mla_kv_body, grid=(m // tm,),
        in_specs=[_rows(tm, d), _resident((1, d)), _resident(p["w_dkv"].shape), _resident((1, KV_LORA)),
                  _resident(p["w_uk"].shape), _resident(p["w_uv_t"].shape), _rows(tm, LANES), _rows(tm, LANES)],
        out_specs=[_rows(tm, hv), _rows(tm, 2 * LANES), pl.BlockSpec((hv, tm), lambda i: (0, i))],
        out_shape=[jax.ShapeDtypeStruct((m, hv), BF16), jax.ShapeDtypeStruct((m, 2 * LANES), BF16),
                   jax.ShapeDtypeStruct((hv, m), BF16)],
        compiler_params=_params("parallel"), name="mla_kv")(
        h, p["kv_norm_g"], p["w_dkv"], p["kv_latent_g"], p["w_uk"], p["w_uv_t"], cos, sin)


def _mla_q_body(h_ref, g_ref, wd_ref, lg_ref, wu_ref, cos_ref, sin_ref, qn_out, qr_out, *, scale):
    hn = _rms(h_ref[...], g_ref[...])
    lat = _rms(_dot(hn, wd_ref[...]), lg_ref[...])
    q = _dot(lat, wu_ref[...]) * scale
    w = qn_out.shape[1]
    wr = qr_out.shape[1]
    cos = jnp.concatenate([cos_ref[...]] * (wr // LANES), axis=1)
    sin = jnp.concatenate([sin_ref[...]] * (wr // LANES), axis=1)
    qn_out[...] = q[:, :w].astype(BF16)
    qr_out[...] = (q[:, w:w + wr] * cos + q[:, w + wr:] * sin).astype(BF16)


def _mla_q(h, g, p, cos, sin):
    m, d = h.shape
    tm = LIGHT_TILE
    w = MLA_HEADS * QK_NOPE
    wr = MLA_HEADS * QK_ROPE
    scale = float(QK_NOPE + QK_ROPE) ** -0.5 * float(np.log2(np.e))
    return pl.pallas_call(
        functools.partial(_mla_q_body, scale=scale), grid=(m // tm,),
        in_specs=[_rows(tm, d), _resident((1, d)), _resident(p["w_dq"].shape), _resident(p["q_latent_g"].shape),
                  _resident(p["w_uq"].shape), _rows(tm, LANES), _rows(tm, LANES)],
        out_specs=[_rows(tm, w), _rows(tm, wr)],
        out_shape=[jax.ShapeDtypeStruct((m, w), BF16), jax.ShapeDtypeStruct((m, wr), BF16)],
        compiler_params=_params("parallel"), name="mla_q")(
        h, g, p["w_dq"], p["q_latent_g"], p["w_uq"], cos, sin)


def _attn_body(qn_ref, qr_ref, kn_ref, kr_ref, vt_ref, o_ref):
    t = ATTN_TILE
    rows = ATTN_CHAIN
    n_chain = t // rows
    n_tiles = qn_ref.shape[0] // t

    def queries(qt, i):
        lo = qt * t + i * rows
        return jnp.concatenate([qn_ref[lo:lo + rows, :], qr_ref[lo:lo + rows, :]], axis=1)

    def keys(kt, n):
        return jnp.concatenate([kn_ref[kt * t:kt * t + n, :], kr_ref[kt * t:kt * t + n, :]], axis=1)

    def n_keys(qt, kt, i):
        return t if kt < qt else (i + 1) * rows

    def scores(qt, kt, i):
        n = n_keys(qt, kt, i)
        s_t = _dot_nt(keys(kt, n), queries(qt, i))
        if kt == qt:
            mask = (lax.broadcasted_iota(jnp.int32, (n, rows), 1) + i * rows
                    >= lax.broadcasted_iota(jnp.int32, (n, rows), 0))
            s_t = jnp.where(mask, s_t, -1e30)
        return s_t

    def update(carry, s_t, v_t):
        m_i, l_i, acc = carry
        m_new = jnp.maximum(m_i, jnp.max(s_t, axis=0, keepdims=True))
        p_t = jnp.exp2(s_t - m_new)
        alpha = jnp.exp2(m_i - m_new)
        l_new = alpha * l_i + jnp.sum(p_t, axis=0, keepdims=True)
        return m_new, l_new, alpha * acc + jnp.dot(v_t, p_t.astype(BF16), preferred_element_type=F32)

    steps = [(qt, kt, i) for qt in range(n_tiles) for kt in range(qt + 1) for i in range(n_chain)]
    one = (jnp.full((1, rows), -jnp.inf, F32), jnp.zeros((1, rows), F32), jnp.zeros((V_HEAD, rows), F32))
    state = [[one] * n_chain for _ in range(n_tiles)]
    ahead = ATTN_LOOKAHEAD
    s_buf = {n: scores(*steps[n]) for n in range(min(ahead, len(steps)))}
    for n, (qt, kt, i) in enumerate(steps):
        if n + ahead < len(steps):
            s_buf[n + ahead] = scores(*steps[n + ahead])
        nk = n_keys(qt, kt, i)
        state[qt][i] = update(state[qt][i], s_buf.pop(n), vt_ref[:, kt * t:kt * t + nk])
        if kt == qt:
            _, l_i, acc = state[qt][i]
            lo = qt * t + i * rows
            o_ref[lo:lo + rows, :] = (acc / l_i).T.astype(BF16)


def _attention(qn, qr, kn, kr, vt, batch, seq):
    m = qn.shape[0]
    spec = pl.BlockSpec((seq, LANES), lambda b, h: (b, h))
    qrspec = pl.BlockSpec((seq, LANES), lambda b, h: (b, h // 2))
    krspec = pl.BlockSpec((seq, LANES), lambda b, h: (b, h % 2))
    vtspec = pl.BlockSpec((V_HEAD, seq), lambda b, h: (h, b))
    return pl.pallas_call(
        _attn_body, grid=(batch, MLA_HEADS),
        in_specs=[spec, qrspec, spec, krspec, vtspec], out_specs=spec,
        out_shape=jax.ShapeDtypeStruct((m, MLA_HEADS * V_HEAD), BF16),
        compiler_params=_params("parallel", "parallel"), name="mla_attention")(qn, qr, kn, kr, vt)


def _prep_rwkv(mix, w_r, w_k, w_v, w_o, w0, w1, w2, a0, a1, a2, g1, g2, k_k, k_a, r_k, gn_w, gn_b):
    d = w_r.shape[0]
    assert (w1.shape[1], a1.shape[1], g1.shape[1]) == (DECAY_LORA, AAA_LORA, GATE_LORA)
    head_of = np.arange(d) // RWKV_HEAD
    e = (head_of[:, None] == np.arange(LANES)[None, :]).astype(np.float32)
    first = jnp.concatenate([w1, a1, g1], axis=1)
    shift = jnp.concatenate([mix[1][:, None] * w1, mix[4][:, None] * a1, mix[5][:, None] * g1], axis=1)
    wl1 = jnp.concatenate([first, shift], axis=0)
    zeros = lambda n: jnp.zeros((n, d), w2.dtype)
    wl2 = jnp.concatenate([
        jnp.concatenate([w2, zeros(AAA_LORA), zeros(GATE_LORA)], axis=0),
        jnp.concatenate([zeros(DECAY_LORA), a2, zeros(GATE_LORA)], axis=0),
        jnp.concatenate([zeros(DECAY_LORA), zeros(AAA_LORA), g2], axis=0)], axis=1)
    return dict(
        mix=mix, wr=w_r.astype(BF16), wk=w_k.astype(BF16), wv=w_v.astype(BF16), wo=w_o.astype(BF16),
        wl1=wl1.astype(BF16), wl2=wl2.astype(BF16), w0=w0[None, :], a0=a0[None, :],
        k_k=k_k[None, :], k_a=k_a[None, :], r_k=r_k.reshape(1, d),
        gn_w=gn_w[None, :], gn_b=gn_b[None, :], e=jnp.asarray(e, BF16), et=jnp.asarray(e.T, BF16),
        et2=jnp.asarray(np.concatenate([e.T, e.T], axis=0), BF16))


def _prep_mla(kv_norm_g, w_dkv, kv_latent_g, w_ukv, w_dq, q_latent_g, w_uq, w_o):
    half = QK_ROPE // 2
    zpad = LANES - QK_ROPE
    x1 = w_dkv[:, KV_LORA:KV_LORA + half]
    x2 = w_dkv[:, KV_LORA + half:]
    z = jnp.zeros((w_dkv.shape[0], zpad), w_dkv.dtype)
    w_dkv_p = jnp.concatenate([w_dkv[:, :KV_LORA], x1, x2, z, x2, x1, z], axis=1)
    ukv = w_ukv.reshape(KV_LORA, MLA_HEADS, QK_NOPE + V_HEAD)
    w_uk = ukv[:, :, :QK_NOPE].reshape(KV_LORA, -1)
    w_uv_t = ukv[:, :, QK_NOPE:].reshape(KV_LORA, -1).T
    uq = w_uq.reshape(w_uq.shape[0], MLA_HEADS, QK_NOPE + QK_ROPE)
    q1 = uq[:, :, QK_NOPE:QK_NOPE + half]
    q2 = uq[:, :, QK_NOPE + half:]
    flat = lambda t: t.reshape(t.shape[0], -1)
    w_uq_p = jnp.concatenate([flat(uq[:, :, :QK_NOPE]), flat(jnp.concatenate([q1, q2], axis=2)),
                              flat(jnp.concatenate([q2, q1], axis=2))], axis=1)
    return dict(kv_norm_g=kv_norm_g[None, :], w_dkv=w_dkv_p.astype(BF16), kv_latent_g=kv_latent_g[None, :],
                w_uk=w_uk.astype(BF16), w_uv_t=w_uv_t.astype(BF16), w_dq=w_dq.astype(BF16), q_latent_g=q_latent_g[None, :],
                w_uq=w_uq_p.astype(BF16), w_o=w_o.astype(BF16))


def _rope_consts():
    half = QK_ROPE // 2
    inv_freq = ROPE_THETA ** (-jnp.arange(0, QK_ROPE, 2, dtype=F32) / QK_ROPE)
    reps = LANES // half
    freq = jnp.tile(inv_freq, reps)[None, :]
    pair = lambda a, b: jnp.tile(jnp.concatenate([jnp.full((half,), a, F32), jnp.full((half,), b, F32)]), reps // 2)[None, :]
    return freq, pair(-1.0, 1.0)


def kernel(x, positions, norm_g, ffn_w_gate, ffn_w_up, ffn_w_down, rwkv_mix, rwkv_w_r, rwkv_w_k, rwkv_w_v, rwkv_w_o, rwkv_w0, rwkv_w1, rwkv_w2, rwkv_a0, rwkv_a1, rwkv_a2, rwkv_g1, rwkv_g2, rwkv_k_k, rwkv_k_a, rwkv_r_k, rwkv_gn_w, rwkv_gn_b, kv_norm_g, mla_w_dkv, mla_kv_latent_g, mla_w_ukv, mla_w_dq, mla_q_latent_g, mla_w_uq, mla_w_o, final_norm_g):
    batch, seq, d = x.shape
    depth = norm_g.shape[0]
    n_a = rwkv_mix.shape[0]
    m = batch * seq
    assert all(seq % t == 0 for t in (TOKEN_TILE, LIGHT_TILE, FFN_TILE, ATTN_TILE, WKV_CHUNK * WKV_SUBCHUNKS))
    assert d % MXU_DIM == 0

    wg, wu, wd = ffn_w_gate.astype(BF16), ffn_w_up.astype(BF16), ffn_w_down.astype(BF16)
    blkmask = jnp.asarray(np.kron(np.eye(WKV_GROUP), np.ones((RWKV_HEAD, RWKV_HEAD))), BF16)
    cos, sin = _rope_tables(positions.reshape(m, 1), *_rope_consts())
    mla = None

    h = x.reshape(m, d)
    kv = None
    for layer in range(depth):
        norm = lambda j: norm_g[layer, j][None, :]
        if layer == n_a:
            mla = _prep_mla(kv_norm_g, mla_w_dkv, mla_kv_latent_g, mla_w_ukv, mla_w_dq[0], mla_q_latent_g[0],
                            mla_w_uq[0], mla_w_o[0])
            kv = _mla_kv(h, mla, cos, sin)
        h = _ffn(h, norm(0), wg, wu, wd, (layer, 0))
        last = layer == depth - 1
        if layer < n_a:
            i = layer
            p = _prep_rwkv(rwkv_mix[i], rwkv_w_r[i], rwkv_w_k[i], rwkv_w_v[i], rwkv_w_o[i], rwkv_w0[i], rwkv_w1[i],
                           rwkv_w2[i], rwkv_a0[i], rwkv_a1[i], rwkv_a2[i], rwkv_g1[i], rwkv_g2[i], rwkv_k_k[i],
                           rwkv_k_a[i], rwkv_r_k[i], rwkv_gn_w[i], rwkv_gn_b[i])
            r, k, v, kk, b, lw, gate, bonus = _rwkv_proj(h, norm(1), p, seq)
            y = _wkv(lw, r, k, v, kk, b, blkmask, batch, seq)
            h = _rwkv_out(y, bonus, gate, h, p)
            pre = None
        else:
            j = layer - n_a
            if j > 0:
                mla = dict(mla, **{n: v_ for n, v_ in _prep_mla(
                    kv_norm_g, mla_w_dkv, mla_kv_latent_g, mla_w_ukv, mla_w_dq[j], mla_q_latent_g[j], mla_w_uq[j],
                    mla_w_o[j]).items() if n in ("w_dq", "q_latent_g", "w_uq", "w_o")})
            qn, qr = _mla_q(h, norm(1), mla, cos, sin)
            o = _attention(qn, qr, kv[0], kv[1], kv[2], batch, seq)
            pre = (o, mla["w_o"])
        h = _ffn(h, norm(2), wg, wu, wd, (layer, 1), pre=pre,
                 final_g=final_norm_g[None, :] if last else None)
    return h.reshape(batch, seq, d)
```
